```python
import jax, jax.numpy as jnp
from jax import lax
import numpy as np

D_MODEL = 2048
BATCH = 2
SEQ = 4096
DEPTH = 4

N_MIXERS = 2
N_POOL_LAYERS = (DEPTH + N_MIXERS - 1) // N_MIXERS
N_SB_LAYERS = DEPTH // N_MIXERS
POOL_WINDOWS = (2, 4, 8, 16)
N_POOL_GROUPS = len(POOL_WINDOWS)
POOL_GROUP_DIM = D_MODEL // N_POOL_GROUPS
HEAD_DIM = 128
N_HEADS = D_MODEL // HEAD_DIM
Q_BLOCK = 128
N_GROUPS = 4
N_EXP_PER_GROUP = 8
TOP_K = 2
D_EXPERT = D_MODEL // 4
NORM_EPS = 1e-6

kernel_name = "hybrid_pool_stickbreak_hmoe"


def rms_norm(x, g):
    xf = x.astype(jnp.float32)
    y = xf * lax.rsqrt(jnp.mean(xf * xf, axis=-1, keepdims=True) + NORM_EPS)
    return (y * g.astype(jnp.float32)).astype(x.dtype)


def modulate(x, g, shift, scale):
    return rms_norm(x, g) * (1 + scale[:, None, :]) + shift[:, None, :]


def multiscale_pool_mixer(h, pool_w, pool_scale):
    b, s, _ = h.shape
    hf = h.astype(jnp.float32).reshape(b, s, N_POOL_GROUPS, POOL_GROUP_DIM)
    cs = jnp.cumsum(hf, axis=1)
    outs = []
    for gi, w in enumerate(POOL_WINDOWS):
        csg = cs[:, :, gi]
        lower = jnp.concatenate(
            [jnp.zeros((b, w, POOL_GROUP_DIM), jnp.float32), csg[:, : s - w]], axis=1)
        count = jnp.minimum(jnp.arange(1, s + 1), w).astype(jnp.float32)
        mean = (csg - lower) / count[None, :, None]
        outs.append(mean - hf[:, :, gi])
    pooled = jnp.stack(outs, axis=2).astype(h.dtype)
    mixed = jnp.einsum('bsgc,gcd->bsgd', pooled, pool_w)
    return mixed.reshape(b, s, D_MODEL) * pool_scale


def stick_breaking_attention(q, k, v):
    s = q.shape[2]
    scale = HEAD_DIM ** -0.5
    outs = []
    for start in range(0, s, Q_BLOCK):
        end = start + Q_BLOCK
        z = jnp.einsum('bhqd,bhkd->bhqk', q[:, :, start:end], k[:, :, :end]).astype(jnp.float32) * scale
        t_idx = start + jnp.arange(Q_BLOCK)[:, None]
        s_idx = jnp.arange(end)[None, :]
        mask = s_idx < t_idx
        log_fail = jnp.where(mask, jax.nn.log_sigmoid(-z), 0.0)
        between = lax.cumsum(log_fail, axis=3, reverse=True) - log_fail
        a = jnp.where(mask, jnp.exp(jax.nn.log_sigmoid(z) + between), 0.0)
        outs.append(jnp.einsum('bhqk,bhkd->bhqd', a.astype(v.dtype), v[:, :, :end]))
    return jnp.concatenate(outs, axis=2)


def stick_breaking_mixer(h, w_qkv, w_o, q_norm_g, k_norm_g):
    b, s, _ = h.shape
    qkv = (h @ w_qkv).reshape(b, s, 3, N_HEADS, HEAD_DIM)
    q = rms_norm(qkv[:, :, 0], q_norm_g).transpose(0, 2, 1, 3)
    k = rms_norm(qkv[:, :, 1], k_norm_g).transpose(0, 2, 1, 3)
    v = qkv[:, :, 2].transpose(0, 2, 1, 3)
    o = stick_breaking_attention(q, k, v)
    return o.transpose(0, 2, 1, 3).reshape(b, s, D_MODEL) @ w_o


def hierarchical_moe(h, w_grp, b_grp, w_exp, b_exp, w_gate, w_up, w_down):
    b, s, d = h.shape
    t = h.reshape(b * s, d)
    grp_logits = (t @ w_grp + b_grp).astype(jnp.float32)
    grp_prob = jax.nn.softmax(grp_logits, axis=-1)
    grp_onehot = jax.nn.one_hot(jnp.argmax(grp_logits, axis=-1), N_GROUPS, dtype=jnp.float32)
    p_grp = jnp.sum(grp_prob * grp_onehot, axis=-1, keepdims=True)
    exp_logits = (jnp.einsum('td,dge->tge', t, w_exp) + b_exp).astype(jnp.float32)
    sel_logits = jnp.einsum('tge,tg->te', exp_logits, grp_onehot)
    top_val, top_idx = lax.top_k(sel_logits, TOP_K)
    top_w = jax.nn.softmax(top_val, axis=-1) * p_grp
    exp_gate = jnp.sum(jax.nn.one_hot(top_idx, N_EXP_PER_GROUP, dtype=jnp.float32) * top_w[..., None], axis=1)
    gates = (grp_onehot[:, :, None] * exp_gate[:, None, :]).astype(h.dtype)
    y = jnp.zeros_like(t)
    for g in range(N_GROUPS):
        a = jnp.einsum('td,edf->tef', t, w_gate[g])
        u = jnp.einsum('td,edf->tef', t, w_up[g])
        hid = jax.nn.silu(a) * u * gates[:, g, :, None]
        y = y + jnp.einsum('tef,efd->td', hid, w_down[g])
    return y.reshape(b, s, d)


def setup_inputs(seed: int = 0) -> dict:
    key = jax.random.key(seed)
    ks = jax.random.split(key, 20)
    n = jax.random.normal
    D, G, E, F = D_MODEL, N_GROUPS, N_EXP_PER_GROUP, D_EXPERT
    return {
        "x": n(ks[0], (BATCH, SEQ, D)),
        "c": n(ks[1], (BATCH, D)),
        "ada_w": n(ks[2], (DEPTH, D, 6 * D)) * (0.5 * D ** -0.5),
        "ada_b": 0.02 * n(ks[3], (DEPTH, 6 * D)),
        "norm1_g": 1.0 + 0.05 * n(ks[4], (DEPTH, D)),
        "norm2_g": 1.0 + 0.05 * n(ks[5], (DEPTH, D)),
        "pool_w": n(ks[6], (N_POOL_LAYERS, N_POOL_GROUPS, POOL_GROUP_DIM, POOL_GROUP_DIM)) * POOL_GROUP_DIM ** -0.5,
        "pool_scale": 1.0 + 0.05 * n(ks[7], (N_POOL_LAYERS, D)),
        "w_qkv": n(ks[8], (N_SB_LAYERS, D, 3 * D)) * D ** -0.5,
        "w_o": n(ks[9], (N_SB_LAYERS, D, D)) * D ** -0.5,
        "q_norm_g": 1.0 + 0.05 * n(ks[10], (N_SB_LAYERS, HEAD_DIM)),
        "k_norm_g": 1.0 + 0.05 * n(ks[11], (N_SB_LAYERS, HEAD_DIM)),
        "w_grp": n(ks[12], (DEPTH, D, G)) * D ** -0.5,
        "b_grp": 0.01 * n(ks[13], (DEPTH, G)),
        "w_exp": n(ks[14], (DEPTH, D, G, E)) * D ** -0.5,
        "b_exp": 0.01 * n(ks[15], (DEPTH, G, E)),
        "w_gate": n(ks[16], (DEPTH, G, E, D, F)) * D ** -0.5,
        "w_up": n(ks[17], (DEPTH, G, E, D, F)) * D ** -0.5,
        "w_down": n(ks[18], (DEPTH, G, E, F, D)) * F ** -0.5,
    }


def reference(x, c, ada_w, ada_b, norm1_g, norm2_g, pool_w, pool_scale, w_qkv, w_o, q_norm_g, k_norm_g,
              w_grp, b_grp, w_exp, b_exp, w_gate, w_up, w_down):
    c_act = jax.nn.silu(c)
    for i in range(DEPTH):
        mod = c_act @ ada_w[i] + ada_b[i]
        shift1, scale1, gate1, shift2, scale2, gate2 = jnp.split(mod, 6, axis=-1)
        j = i // N_MIXERS
        h = modulate(x, norm1_g[i], shift1, scale1)
        if i % N_MIXERS == 0:
            mix = multiscale_pool_mixer(h, pool_w[j], pool_scale[j])
        else:
            mix = stick_breaking_mixer(h, w_qkv[j], w_o[j], q_norm_g[j], k_norm_g[j])
        x = x + gate1[:, None, :] * mix
        h = modulate(x, norm2_g[i], shift2, scale2)
        x = x + gate2[:, None, :] * hierarchical_moe(h, w_grp[i], b_grp[i], w_exp[i], b_exp[i],
                                                      w_gate[i], w_up[i], w_down[i])
    return x
```

```python
import functools

import jax
import jax.numpy as jnp
from jax import lax
from jax.experimental import pallas as pl
from jax.experimental.pallas import tpu as pltpu

F32 = jnp.float32
BF16 = jnp.bfloat16
I32 = jnp.int32

NORM_EPS = 1e-6
POOL_WINDOWS = (2, 4, 8, 16)
MAX_WINDOW = max(POOL_WINDOWS)
HEAD_DIM = 128
LANES = 128
VMEM_LIMIT_BYTES = 56 * 1024 * 1024
LOG_F32_ZERO = -110.0


def _params(*sem):
    return pltpu.CompilerParams(dimension_semantics=sem, vmem_limit_bytes=VMEM_LIMIT_BYTES)


def _modulate(x, g, shift, scale):
    y = x * lax.rsqrt(jnp.mean(x * x, axis=-1, keepdims=True) + NORM_EPS)
    return (y * g) * (1.0 + scale) + shift


def _ada_kernel(c_ref, w_ref, b_ref, o_ref):
    c = c_ref[...]
    c_act = c * (1.0 / (1.0 + jnp.exp(-c)))
    o_ref[0] = jnp.dot(c_act.astype(BF16), w_ref[0].astype(BF16),
                       preferred_element_type=F32) + b_ref[0]


def _ada_call(c, ada_w, ada_b):
    depth, d, n = ada_w.shape
    b = c.shape[0]
    tn = min(n, 1024)
    return pl.pallas_call(
        _ada_kernel,
        grid=(depth, n // tn),
        in_specs=[pl.BlockSpec((b, d), lambda l, j: (0, 0)),
                  pl.BlockSpec((1, d, tn), lambda l, j: (l, 0, j)),
                  pl.BlockSpec((1, 1, tn), lambda l, j: (l, 0, j))],
        out_specs=pl.BlockSpec((1, b, tn), lambda l, j: (l, 0, j)),
        out_shape=jax.ShapeDtypeStruct((depth, b, n), F32),
        compiler_params=_params("arbitrary", "arbitrary"),
        name="ada",
    )(c, ada_w, ada_b.reshape(depth, 1, n))


def _pool_kernel(x_ref, sh_ref, sc_ref, gt_ref, g_ref, pw_ref, ps_ref, o_ref, hext_ref, *, ts, dg):
    s = pl.program_id(1)
    x = x_ref[0]
    h = _modulate(x, g_ref[...], sh_ref[0], sc_ref[0])

    @pl.when(s == 0)
    def _():
        hext_ref[0:MAX_WINDOW, :] = jnp.zeros((MAX_WINDOW, h.shape[1]), F32)

    @pl.when(s > 0)
    def _():
        hext_ref[0:MAX_WINDOW, :] = hext_ref[ts:ts + MAX_WINDOW, :]

    hext_ref[MAX_WINDOW:MAX_WINDOW + ts, :] = h
    pos = s * ts + lax.broadcasted_iota(I32, (ts, dg), 0)
    for gi, w in enumerate(POOL_WINDOWS):
        c0 = gi * dg
        acc = h[:, c0:c0 + dg]
        for k in range(1, w):
            acc = acc + hext_ref[MAX_WINDOW - k:MAX_WINDOW - k + ts, c0:c0 + dg]
        count = jnp.minimum(pos + 1, w).astype(F32)
        pooled = acc / count - h[:, c0:c0 + dg]
        mixed = jnp.dot(pooled.astype(BF16), pw_ref[gi], preferred_element_type=F32)
        o_ref[0, :, c0:c0 + dg] = (x[:, c0:c0 + dg]
                                   + gt_ref[0][:, c0:c0 + dg] * (mixed * ps_ref[:, c0:c0 + dg]))


def _pool_call(x, shift, scale, gate, norm_g, pool_w, pool_scale):
    b, s, d = x.shape
    ng, dg, _ = pool_w.shape
    ts = min(s, 256)
    vec = pl.BlockSpec((1, 1, d), lambda bi, si: (bi, 0, 0))
    row = pl.BlockSpec((1, d), lambda bi, si: (0, 0))
    return pl.pallas_call(
        functools.partial(_pool_kernel, ts=ts, dg=dg),
        grid=(b, s // ts),
        in_specs=[pl.BlockSpec((1, ts, d), lambda bi, si: (bi, si, 0)), vec, vec, vec, row,
                  pl.BlockSpec((ng, dg, dg), lambda bi, si: (0, 0, 0)), row],
        out_specs=pl.BlockSpec((1, ts, d), lambda bi, si: (bi, si, 0)),
        out_shape=jax.ShapeDtypeStruct((b, s, d), F32),
        scratch_shapes=[pltpu.VMEM((ts + MAX_WINDOW, d), F32)],
        compiler_params=_params("arbitrary", "arbitrary"),
        name="pool_mixer",
    )(x, shift, scale, gate, norm_g.reshape(1, d), pool_w.astype(BF16), pool_scale.reshape(1, d))


def _qkv_kernel(x_ref, sh_ref, sc_ref, g_ref, w_ref, ng_ref, o_ref, h_ref, *, n_norm_tiles, heads):
    j = pl.program_id(2)

    @pl.when(j == 0)
    def _():
        h_ref[...] = _modulate(x_ref[0], g_ref[...], sh_ref[0], sc_ref[0]).astype(BF16)

    res = jnp.dot(h_ref[...], w_ref[...], preferred_element_type=F32)

    @pl.when(j < n_norm_tiles)
    def _():
        g = ng_ref[0]
        for hh in range(heads):
            blk = res[:, hh * HEAD_DIM:(hh + 1) * HEAD_DIM]
            y = blk * lax.rsqrt(jnp.mean(blk * blk, axis=-1, keepdims=True) + NORM_EPS)
            o_ref[hh, 0] = (y * g).astype(BF16)

    @pl.when(j >= n_norm_tiles)
    def _():
        for hh in range(heads):
            o_ref[hh, 0] = res[:, hh * HEAD_DIM:(hh + 1) * HEAD_DIM].astype(BF16)


def _qkv_call(x, shift, scale, norm_g, w_qkv, q_norm_g, k_norm_g):
    b, s, d = x.shape
    n = w_qkv.shape[1]
    tm = min(s, 512)
    tn = min(d, 512)
    heads = tn // HEAD_DIM
    tiles_per_part = d // tn
    qk_gain = jnp.stack([q_norm_g * (HEAD_DIM ** -0.5), k_norm_g, jnp.ones_like(k_norm_g)])
    vec = pl.BlockSpec((1, 1, d), lambda bi, si, j: (bi, 0, 0))
    return pl.pallas_call(
        functools.partial(_qkv_kernel, n_norm_tiles=2 * tiles_per_part, heads=heads),
        grid=(b, s // tm, n // tn),
        in_specs=[pl.BlockSpec((1, tm, d), lambda bi, si, j: (bi, si, 0)), vec, vec,
                  pl.BlockSpec((1, d), lambda bi, si, j: (0, 0)),
                  pl.BlockSpec((d, tn), lambda bi, si, j: (0, j)),
                  pl.BlockSpec((1, 1, HEAD_DIM), lambda bi, si, j: (j // tiles_per_part, 0, 0))],
        out_specs=pl.BlockSpec((heads, 1, tm, HEAD_DIM), lambda bi, si, j: (j, bi, si, 0)),
        out_shape=jax.ShapeDtypeStruct((n // HEAD_DIM, b, s, HEAD_DIM), BF16),
        scratch_shapes=[pltpu.VMEM((tm, d), BF16)],
        compiler_params=_params("arbitrary", "arbitrary", "arbitrary"),
        name="qkv_proj",
    )(x, shift, scale, norm_g.reshape(1, d), w_qkv.astype(BF16), qk_gain.reshape(3, 1, HEAD_DIM))


def _attn_kernel(q_ref, k_ref, v_ref, o_ref, *, tq):
    qi = pl.program_id(2)
    q = q_ref[0, 0]
    row = lax.broadcasted_iota(I32, (tq, tq), 0)
    col = lax.broadcasted_iota(I32, (tq, tq), 1)
    suffix_sum = (row >= col).astype(BF16)
    causal = col < row

    def block(j, acc, r, masked):
        start = pl.multiple_of(j * tq, tq)
        k = k_ref[0, 0, pl.ds(start, tq), :]
        v = v_ref[0, 0, pl.ds(start, tq), :]
        z = lax.dot_general(q, k, (((1,), (1,)), ((), ())), preferred_element_type=F32)
        log_fail = -(jnp.maximum(z, 0.0) + jnp.log(1.0 + jnp.exp(-jnp.abs(z))))
        if masked:
            log_fail = jnp.where(causal, log_fail, 0.0)
        cum = jnp.dot(log_fail.astype(BF16), suffix_sum, preferred_element_type=F32)
        a = jnp.exp(z + cum + r)
        if masked:
            a = jnp.where(causal, a, 0.0)
        acc = acc + jnp.dot(a.astype(BF16), v, preferred_element_type=F32)
        return acc, r + cum[:, 0:1]

    acc, r = block(qi, jnp.zeros((tq, HEAD_DIM), F32), jnp.zeros((tq, 1), F32), True)

    def cond(carry):
        j, _, _, r_max = carry
        return jnp.logical_and(j >= 0, r_max > LOG_F32_ZERO)

    def body(carry):
        j, acc, r, _ = carry
        acc, r = block(j, acc, r, False)
        return j - 1, acc, r, jnp.max(r)

    _, acc, _, _ = lax.while_loop(cond, body, (qi - 1, acc, r, jnp.max(r)))
    o_ref[0] = acc.astype(o_ref.dtype)


def _attn_call(qkv, d):
    n3h, b, s, _ = qkv.shape
    nh = n3h // 3
    tq = min(s, 256)
    return pl.pallas_call(
        functools.partial(_attn_kernel, tq=tq),
        grid=(b, nh, s // tq),
        in_specs=[pl.BlockSpec((1, 1, tq, HEAD_DIM), lambda bi, h, qi: (h, bi, qi, 0)),
                  pl.BlockSpec((1, 1, s, HEAD_DIM), lambda bi, h, qi: (nh + h, bi, 0, 0)),
                  pl.BlockSpec((1, 1, s, HEAD_DIM), lambda bi, h, qi: (2 * nh + h, bi, 0, 0))],
        out_specs=pl.BlockSpec((1, tq, HEAD_DIM), lambda bi, h, qi: (bi, qi, h)),
        out_shape=jax.ShapeDtypeStruct((b, s, d), BF16),
        compiler_params=_params("arbitrary", "arbitrary", "arbitrary"),
        name="stickbreak_attn",
    )(qkv, qkv, qkv)


def _oproj_kernel(o_ref, w_ref, x_ref, gt_ref, out_ref):
    out_ref[0] = x_ref[0] + gt_ref[0] * jnp.dot(o_ref[0], w_ref[...], preferred_element_type=F32)


def _oproj_call(o, w_o, x, gate):
    b, s, d = x.shape
    tm = min(s, 512)
    tn = min(d, 512)
    return pl.pallas_call(
        _oproj_kernel,
        grid=(b, s // tm, d // tn),
        in_specs=[pl.BlockSpec((1, tm, d), lambda bi, si, j: (bi, si, 0)),
                  pl.BlockSpec((d, tn), lambda bi, si, j: (0, j)),
                  pl.BlockSpec((1, tm, tn), lambda bi, si, j: (bi, si, j)),
                  pl.BlockSpec((1, 1, tn), lambda bi, si, j: (bi, 0, j))],
        out_specs=pl.BlockSpec((1, tm, tn), lambda bi, si, j: (bi, si, j)),
        out_shape=jax.ShapeDtypeStruct((b, s, d), F32),
        compiler_params=_params("arbitrary", "arbitrary", "arbitrary"),
        name="attn_out_proj",
    )(o, w_o.astype(BF16), x, gate)


def _split_bf16(a):
    hi = a.astype(BF16)
    return hi, (a - hi.astype(F32)).astype(BF16)


def _router_kernel(x_ref, sh_ref, sc_ref, g_ref, wr_ref, br_ref, h_ref, route_ref, cnt_ref,
                   *, ts, n_groups, n_exp):
    first = jnp.logical_and(pl.program_id(0) == 0, pl.program_id(1) == 0)

    @pl.when(first)
    def _():
        cnt_ref[...] = jnp.zeros_like(cnt_ref)

    h = _modulate(x_ref[0], g_ref[...], sh_ref[0], sc_ref[0])
    h_ref[0] = h

    h_hi, h_lo = _split_bf16(h)
    w_hi, w_lo = _split_bf16(wr_ref[...])
    logits = (jnp.dot(h_hi, w_hi, preferred_element_type=F32)
              + jnp.dot(h_hi, w_lo, preferred_element_type=F32)
              + jnp.dot(h_lo, w_hi, preferred_element_type=F32)) + br_ref[...]

    lane = lax.broadcasted_iota(I32, (ts, LANES), 1)
    neg = jnp.float32(-jnp.inf)

    def first_argmax(vals):
        m = jnp.max(vals, axis=-1, keepdims=True)
        idx = jnp.min(jnp.where(vals == m, lane, LANES), axis=-1, keepdims=True)
        return m, idx

    is_grp = lane < n_groups
    g_max, g_idx = first_argmax(jnp.where(is_grp, logits, neg))
    g_sum = jnp.sum(jnp.where(is_grp, jnp.exp(logits - g_max), 0.0), axis=-1, keepdims=True)
    p_grp = 1.0 / g_sum
    lo = n_groups + g_idx * n_exp
    in_grp = jnp.logical_and(lane >= lo, lane < lo + n_exp)
    sel = jnp.where(in_grp, logits, neg)
    v1, i1 = first_argmax(sel)
    v2, i2 = first_argmax(jnp.where(lane == i1, neg, sel))
    e21 = jnp.exp(v2 - v1)
    w1 = p_grp * (1.0 / (1.0 + e21))
    w2 = p_grp * (e21 / (1.0 + e21))

    oh1 = (lane == i1).astype(F32)
    oh2 = (lane == i2).astype(F32)
    both = oh1 + oh2
    earlier = (lax.broadcasted_iota(I32, (ts, ts), 1) < lax.broadcasted_iota(I32, (ts, ts), 0))
    ranks = jnp.dot(earlier.astype(BF16), both.astype(BF16), preferred_element_type=F32) + cnt_ref[...]
    r1 = jnp.sum(oh1 * ranks, axis=-1, keepdims=True)
    r2 = jnp.sum(oh2 * ranks, axis=-1, keepdims=True)
    cnt_ref[...] = cnt_ref[...] + jnp.sum(both, axis=0, keepdims=True)

    out = jnp.zeros((ts, LANES), F32)
    for k, val in enumerate(((i1 - n_groups).astype(F32), (i2 - n_groups).astype(F32), w1, w2, r1, r2)):
        out = jnp.where(lane == k, val, out)
    route_ref[0] = out


def _router_call(x, shift, scale, norm_g, w_grp, b_grp, w_exp, b_exp):
    b, s, d = x.shape
    n_groups, n_exp = b_exp.shape
    n_logits = n_groups + n_groups * n_exp
    ts = min(s, 256)
    w_r = jnp.concatenate([w_grp, w_exp.reshape(d, n_groups * n_exp),
                           jnp.zeros((d, LANES - n_logits), F32)], axis=1)
    b_r = jnp.concatenate([b_grp, b_exp.reshape(-1), jnp.zeros((LANES - n_logits,), F32)]).reshape(1, LANES)
    vec = pl.BlockSpec((1, 1, d), lambda bi, si: (bi, 0, 0))
    return pl.pallas_call(
        functools.partial(_router_kernel, ts=ts, n_groups=n_groups, n_exp=n_exp),
        grid=(b, s // ts),
        in_specs=[pl.BlockSpec((1, ts, d), lambda bi, si: (bi, si, 0)), vec, vec,
                  pl.BlockSpec((1, d), lambda bi, si: (0, 0)),
                  pl.BlockSpec((d, LANES), lambda bi, si: (0, 0)),
                  pl.BlockSpec((1, LANES), lambda bi, si: (0, 0))],
        out_specs=[pl.BlockSpec((1, ts, d), lambda bi, si: (bi, si, 0)),
                   pl.BlockSpec((1, ts, LANES), lambda bi, si: (bi, si, 0)),
                   pl.BlockSpec((1, LANES), lambda bi, si: (0, 0))],
        out_shape=[jax.ShapeDtypeStruct((b, s, d), F32),
                   jax.ShapeDtypeStruct((b, s, LANES), F32),
                   jax.ShapeDtypeStruct((1, LANES), F32)],
        compiler_params=_params("arbitrary", "arbitrary"),
        name="moe_router",
    )(x, shift, scale, norm_g.reshape(1, d), w_r, b_r)


def _row_gather_start(src_hbm, dst, sem, idx_ref, base, n_rows):
    def body(r, carry):
        pltpu.make_async_copy(src_hbm.at[pl.ds(idx_ref[base + r], 1)], dst.at[pl.ds(r, 1)], sem).start()
        return carry
    lax.fori_loop(0, n_rows, body, 0)


def _row_gather_wait(src_hbm, dst, sem, n_rows):
    def body(r, carry):
        pltpu.make_async_copy(src_hbm.at[pl.ds(0, 1)], dst.at[pl.ds(r, 1)], sem).wait()
        return carry
    lax.fori_loop(0, n_rows, body, 0)


def _expert_kernel(tok_ref, te_ref, nact_ref, h_hbm, wg_ref, wu_ref, wd_ref, y_ref,
                   xbuf, sem, wg_bf, wu_bf, wd_bf, *, tm):
    t = pl.program_id(0)
    n_active = nact_ref[0]

    @pl.when(t == 0)
    def _():
        _row_gather_start(h_hbm, xbuf.at[0], sem.at[0], tok_ref, 0, tm)

    @pl.when(t + 1 < n_active)
    def _():
        nxt = (t + 1) % 2
        _row_gather_start(h_hbm, xbuf.at[nxt], sem.at[nxt], tok_ref, (t + 1) * tm, tm)

    @pl.when(t < n_active)
    def _():
        slot = t % 2
        new_expert = jnp.logical_or(t == 0, te_ref[t] != te_ref[jnp.maximum(t - 1, 0)])

        @pl.when(new_expert)
        def _():
            wg_bf[...] = wg_ref[0].astype(BF16)
            wu_bf[...] = wu_ref[0].astype(BF16)
            wd_bf[...] = wd_ref[0].astype(BF16)

        _row_gather_wait(h_hbm, xbuf.at[slot], sem.at[slot], tm)
        x = xbuf[slot].astype(BF16)
        a = jnp.dot(x, wg_bf[...], preferred_element_type=F32)
        u = jnp.dot(x, wu_bf[...], preferred_element_type=F32)
        hid = (a * (1.0 / (1.0 + jnp.exp(-a)))) * u
        y_ref[...] = jnp.dot(hid.astype(BF16), wd_bf[...], preferred_element_type=F32)

    @pl.when(t >= n_active)
    def _():
        y_ref[...] = jnp.zeros_like(y_ref)


def _expert_call(h2, tok_of_pos, tile_expert, n_active, w_gate, w_up, w_down, layer, tm):
    t, d = h2.shape
    n_all, _, f = w_gate.shape
    n_tiles = tile_expert.shape[0]
    experts_per_layer = n_all // layer[1]
    base = layer[0] * experts_per_layer

    def w_map(ti, tok, te, na):
        return (base + te[ti], 0, 0)

    grid_spec = pltpu.PrefetchScalarGridSpec(
        num_scalar_prefetch=3,
        grid=(n_tiles,),
        in_specs=[pl.BlockSpec(memory_space=pl.ANY),
                  pl.BlockSpec((1, d, f), w_map),
                  pl.BlockSpec((1, d, f), w_map),
                  pl.BlockSpec((1, f, d), w_map)],
        out_specs=pl.BlockSpec((tm, d), lambda ti, tok, te, na: (ti, 0)),
        scratch_shapes=[pltpu.VMEM((2, tm, d), F32),
                        pltpu.SemaphoreType.DMA((2,)),
                        pltpu.VMEM((d, f), BF16), pltpu.VMEM((d, f), BF16), pltpu.VMEM((f, d), BF16)],
    )
    return pl.pallas_call(
        functools.partial(_expert_kernel, tm=tm),
        grid_spec=grid_spec,
        out_shape=jax.ShapeDtypeStruct((n_tiles * tm, d), F32),
        compiler_params=_params("arbitrary"),
        name="moe_experts",
    )(tok_of_pos, tile_expert, n_active, h2, w_gate, w_up, w_down)


def _combine_kernel(p1_ref, p2_ref, x_ref, route_ref, gt_ref, y_hbm, o_ref, ybuf, sem, *, ts, tiles_per_seq):
    b = pl.program_id(0)
    s = pl.program_id(1)
    step = b * tiles_per_seq + s
    n_steps = pl.num_programs(0) * tiles_per_seq

    def start(tile, slot):
        _row_gather_start(y_hbm, ybuf.at[slot, 0], sem.at[slot], p1_ref, tile * ts, ts)
        _row_gather_start(y_hbm, ybuf.at[slot, 1], sem.at[slot], p2_ref, tile * ts, ts)

    @pl.when(step == 0)
    def _():
        start(0, 0)

    @pl.when(step + 1 < n_steps)
    def _():
        start(step + 1, (step + 1) % 2)

    slot = step % 2
    _row_gather_wait(y_hbm, ybuf.at[slot, 0], sem.at[slot], ts)
    _row_gather_wait(y_hbm, ybuf.at[slot, 1], sem.at[slot], ts)
    route = route_ref[0]
    moe = route[:, 2:3] * ybuf[slot, 0] + route[:, 3:4] * ybuf[slot, 1]
    o_ref[0] = x_ref[0] + gt_ref[0] * moe


def _combine_call(x, route, gate, y, pos1, pos2):
    b, s, d = x.shape
    ts = min(s, 256)
    tiles_per_seq = s // ts
    grid_spec = pltpu.PrefetchScalarGridSpec(
        num_scalar_prefetch=2,
        grid=(b, tiles_per_seq),
        in_specs=[pl.BlockSpec((1, ts, d), lambda bi, si, p1, p2: (bi, si, 0)),
                  pl.BlockSpec((1, ts, LANES), lambda bi, si, p1, p2: (bi, si, 0)),
                  pl.BlockSpec((1, 1, d), lambda bi, si, p1, p2: (bi, 0, 0)),
                  pl.BlockSpec(memory_space=pl.ANY)],
        out_specs=pl.BlockSpec((1, ts, d), lambda bi, si, p1, p2: (bi, si, 0)),
        scratch_shapes=[pltpu.VMEM((2, 2, ts, d), F32), pltpu.SemaphoreType.DMA((2,))],
    )
    return pl.pallas_call(
        functools.partial(_combine_kernel, ts=ts, tiles_per_seq=tiles_per_seq),
        grid_spec=grid_spec,
        out_shape=jax.ShapeDtypeStruct((b, s, d), F32),
        compiler_params=_params("arbitrary", "arbitrary"),
        name="moe_combine",
    )(pos1, pos2, x, route, gate, y)


def _moe(x, shift, scale, gate, norm_g, w_grp, b_grp, w_exp, b_exp, w_gate, w_up, w_down, layer):
    b, s, d = x.shape
    t = b * s
    n_groups, n_exp = b_exp.shape
    n_experts = n_groups * n_exp
    tm = min(t, 256)

    h2, route, counts = _router_call(x, shift, scale, norm_g, w_grp, b_grp, w_exp, b_exp)

    route_t = route.reshape(t, LANES)
    e1 = route_t[:, 0].astype(I32)
    e2 = route_t[:, 1].astype(I32)
    r1 = route_t[:, 4].astype(I32)
    r2 = route_t[:, 5].astype(I32)
    cnt = counts[0, n_groups:n_groups + n_experts].astype(I32)
    tiles_e = (cnt + tm - 1) // tm
    tiles_end = jnp.cumsum(tiles_e)
    offset = (tiles_end - tiles_e) * tm
    pos1 = offset[e1] + r1
    pos2 = offset[e2] + r2
    n_tiles = (2 * t + n_experts * (tm - 1) + tm - 1) // tm
    tok = jnp.arange(t, dtype=I32)
    tok_of_pos = jnp.zeros((n_tiles * tm,), I32).at[pos1].set(tok).at[pos2].set(tok)
    n_active = tiles_end[-1:]
    tile_ids = jnp.minimum(jnp.arange(n_tiles, dtype=I32), n_active[0] - 1)
    tile_expert = jnp.searchsorted(tiles_end, tile_ids, side="right").astype(I32)

    y = _expert_call(h2.reshape(t, d), tok_of_pos, tile_expert, n_active.astype(I32),
                     w_gate, w_up, w_down, layer, tm)
    return _combine_call(x, route, gate, y, pos1, pos2)


def kernel(x, c, ada_w, ada_b, norm1_g, norm2_g, pool_w, pool_scale, w_qkv, w_o, q_norm_g, k_norm_g,
           w_grp, b_grp, w_exp, b_exp, w_gate, w_up, w_down):
    depth = ada_w.shape[0]
    b, s, d = x.shape
    n_mixers = 2
    mod = _ada_call(c, ada_w, ada_b).reshape(depth, b, 6, 1, d)
    wshape = w_gate.shape
    w_gate_s = w_gate.reshape((-1,) + wshape[3:])
    w_up_s = w_up.reshape((-1,) + wshape[3:])
    w_down_s = w_down.reshape((-1,) + w_down.shape[3:])
    for i in range(depth):
        shift1, scale1, gate1, shift2, scale2, gate2 = (mod[i, :, k] for k in range(6))
        j = i // n_mixers
        if i % n_mixers == 0:
            x = _pool_call(x, shift1, scale1, gate1, norm1_g[i], pool_w[j], pool_scale[j])
        else:
            qkv = _qkv_call(x, shift1, scale1, norm1_g[i], w_qkv[j], q_norm_g[j], k_norm_g[j])
            o = _attn_call(qkv, d)
            x = _oproj_call(o, w_o[j], x, gate1)
        x = _moe(x, shift2, scale2, gate2, norm2_g[i], w_grp[i], b_grp[i], w_exp[i], b_exp[i],
                 w_gate_s, w_up_s, w_down_s, (i, depth))
    return x
```

```python
import functools

import jax
import jax.numpy as jnp
from jax import lax
from jax.experimental import pallas as pl
from jax.experimental.pallas import tpu as pltpu

F32 = jnp.float32
BF16 = jnp.bfloat16
I32 = jnp.int32

NORM_EPS = 1e-6
POOL_WINDOWS = (2, 4, 8, 16)
MAX_WINDOW = max(POOL_WINDOWS)
HEAD_DIM = 128
LANES = 128
ROUTE_FIELDS = 8
DMA_ISSUE_UNROLL = 8
VMEM_LIMIT_BYTES = 56 * 1024 * 1024
LOG_F32_ZERO = -110.0


def _params(*sem):
    return pltpu.CompilerParams(dimension_semantics=sem, vmem_limit_bytes=VMEM_LIMIT_BYTES)


def _modulate(x, g, shift, scale):
    y = x * lax.rsqrt(jnp.mean(x * x, axis=-1, keepdims=True) + NORM_EPS)
    return (y * g) * (1.0 + scale) + shift


def _ada_kernel(c_ref, w_ref, b_ref, o_ref):
    c = c_ref[...]
    c_act = c * (1.0 / (1.0 + jnp.exp(-c)))
    o_ref[0] = jnp.dot(c_act.astype(BF16), w_ref[0].astype(BF16),
                       preferred_element_type=F32) + b_ref[0]


def _ada_call(c, ada_w, ada_b):
    depth, d, n = ada_w.shape
    b = c.shape[0]
    tn = min(n, 1024)
    return pl.pallas_call(
        _ada_kernel,
        grid=(depth, n // tn),
        in_specs=[pl.BlockSpec((b, d), lambda l, j: (0, 0)),
                  pl.BlockSpec((1, d, tn), lambda l, j: (l, 0, j)),
                  pl.BlockSpec((1, 1, tn), lambda l, j: (l, 0, j))],
        out_specs=pl.BlockSpec((1, b, tn), lambda l, j: (l, 0, j)),
        out_shape=jax.ShapeDtypeStruct((depth, b, n), F32),
        compiler_params=_params("arbitrary", "arbitrary"),
        name="ada",
    )(c, ada_w, ada_b.reshape(depth, 1, n))


def _pool_kernel(x_ref, sh_ref, sc_ref, gt_ref, g_ref, pw_ref, ps_ref, o_ref, hext_ref, *, ts, dg):
    s = pl.program_id(1)
    x = x_ref[0]
    h = _modulate(x, g_ref[...], sh_ref[0], sc_ref[0])

    @pl.when(s == 0)
    def _():
        hext_ref[0:MAX_WINDOW, :] = jnp.zeros((MAX_WINDOW, h.shape[1]), F32)

    @pl.when(s > 0)
    def _():
        hext_ref[0:MAX_WINDOW, :] = hext_ref[ts:ts + MAX_WINDOW, :]

    hext_ref[MAX_WINDOW:MAX_WINDOW + ts, :] = h
    pos = s * ts + lax.broadcasted_iota(I32, (ts, dg), 0)
    for gi, w in enumerate(POOL_WINDOWS):
        c0 = gi * dg
        acc = h[:, c0:c0 + dg]
        for k in range(1, w):
            acc = acc + hext_ref[MAX_WINDOW - k:MAX_WINDOW - k + ts, c0:c0 + dg]
        count = jnp.minimum(pos + 1, w).astype(F32)
        pooled = acc / count - h[:, c0:c0 + dg]
        mixed = jnp.dot(pooled.astype(BF16), pw_ref[gi], preferred_element_type=F32)
        o_ref[0, :, c0:c0 + dg] = (x[:, c0:c0 + dg]
                                   + gt_ref[0][:, c0:c0 + dg] * (mixed * ps_ref[:, c0:c0 + dg]))


def _pool_call(x, shift, scale, gate, norm_g, pool_w, pool_scale):
    b, s, d = x.shape
    ng, dg, _ = pool_w.shape
    ts = min(s, 256)
    vec = pl.BlockSpec((1, 1, d), lambda bi, si: (bi, 0, 0))
    row = pl.BlockSpec((1, d), lambda bi, si: (0, 0))
    return pl.pallas_call(
        functools.partial(_pool_kernel, ts=ts, dg=dg),
        grid=(b, s // ts),
        in_specs=[pl.BlockSpec((1, ts, d), lambda bi, si: (bi, si, 0)), vec, vec, vec, row,
                  pl.BlockSpec((ng, dg, dg), lambda bi, si: (0, 0, 0)), row],
        out_specs=pl.BlockSpec((1, ts, d), lambda bi, si: (bi, si, 0)),
        out_shape=jax.ShapeDtypeStruct((b, s, d), F32),
        scratch_shapes=[pltpu.VMEM((ts + MAX_WINDOW, d), F32)],
        compiler_params=_params("arbitrary", "arbitrary"),
        name="pool_mixer",
    )(x, shift, scale, gate, norm_g.reshape(1, d), pool_w.astype(BF16), pool_scale.reshape(1, d))


def _qkv_kernel(x_ref, sh_ref, sc_ref, g_ref, w_ref, ng_ref, o_ref, h_ref, res_ref, *, n_norm_tiles, heads):
    j = pl.program_id(2)
    tm = h_ref.shape[0]
    rows = min(tm, 256)
    sub = min(rows, 64)

    @pl.when(j == 0)
    def _():
        h_ref[...] = _modulate(x_ref[0], g_ref[...], sh_ref[0], sc_ref[0]).astype(BF16)

    is_norm = j < n_norm_tiles
    g = ng_ref[0]
    for r0 in range(0, tm, rows):
        res_ref[r0:r0 + rows, :] = jnp.dot(h_ref[r0:r0 + rows, :], w_ref[...], preferred_element_type=F32)
        for hh in range(heads):
            for r1 in range(r0, r0 + rows, sub):
                blk = res_ref[r1:r1 + sub, hh * HEAD_DIM:(hh + 1) * HEAD_DIM]
                inv = lax.rsqrt(jnp.mean(blk * blk, axis=-1, keepdims=True) + NORM_EPS)
                y = jnp.where(is_norm, (blk * inv) * g, blk)
                o_ref[hh, 0, r1:r1 + sub, :] = y.astype(BF16)


def _qkv_call(x, shift, scale, norm_g, w_qkv, q_norm_g, k_norm_g):
    b, s, d = x.shape
    n = w_qkv.shape[1]
    tm = min(s, 512)
    tn = min(d, 1024)
    heads = tn // HEAD_DIM
    tiles_per_part = d // tn
    qk_gain = jnp.stack([q_norm_g * (HEAD_DIM ** -0.5), k_norm_g, jnp.ones_like(k_norm_g)])
    vec = pl.BlockSpec((1, 1, d), lambda bi, si, j: (bi, 0, 0))
    return pl.pallas_call(
        functools.partial(_qkv_kernel, n_norm_tiles=2 * tiles_per_part, heads=heads),
        grid=(b, s // tm, n // tn),
        in_specs=[pl.BlockSpec((1, tm, d), lambda bi, si, j: (bi, si, 0)), vec, vec,
                  pl.BlockSpec((1, d), lambda bi, si, j: (0, 0)),
                  pl.BlockSpec((d, tn), lambda bi, si, j: (0, j)),
                  pl.BlockSpec((1, 1, HEAD_DIM), lambda bi, si, j: (j // tiles_per_part, 0, 0))],
        out_specs=pl.BlockSpec((heads, 1, tm, HEAD_DIM), lambda bi, si, j: (j, bi, si, 0)),
        out_shape=jax.ShapeDtypeStruct((n // HEAD_DIM, b, s, HEAD_DIM), BF16),
        scratch_shapes=[pltpu.VMEM((tm, d), BF16), pltpu.VMEM((tm, tn), F32)],
        compiler_params=_params("arbitrary", "arbitrary", "arbitrary"),
        name="qkv_proj",
    )(x, shift, scale, norm_g.reshape(1, d), w_qkv.astype(BF16), qk_gain.reshape(3, 1, HEAD_DIM))


def _attn_kernel(q_ref, k_ref, v_ref, o_ref, *, tq, heads):
    qi = pl.program_id(2)
    row = lax.broadcasted_iota(I32, (tq, tq), 0)
    col = lax.broadcasted_iota(I32, (tq, tq), 1)
    suffix_sum = (row >= col).astype(BF16)
    causal = col < row

    def block(j, accs, rs, masked):
        start = pl.multiple_of(j * tq, tq)
        new_accs, new_rs = [], []
        for g in range(heads):
            k = k_ref[g, 0, pl.ds(start, tq), :]
            v = v_ref[g, 0, pl.ds(start, tq), :]
            z = lax.dot_general(q_ref[g, 0], k, (((1,), (1,)), ((), ())), preferred_element_type=F32)
            log_fail = -(jnp.maximum(z, 0.0) + jnp.log(1.0 + jnp.exp(-jnp.abs(z))))
            if masked:
                log_fail = jnp.where(causal, log_fail, 0.0)
            cum = jnp.dot(log_fail.astype(BF16), suffix_sum, preferred_element_type=F32)
            a = jnp.exp(z + cum + rs[g])
            if masked:
                a = jnp.where(causal, a, 0.0)
            new_accs.append(accs[g] + jnp.dot(a.astype(BF16), v, preferred_element_type=F32))
            new_rs.append(rs[g] + cum[:, 0:1])
        return tuple(new_accs), tuple(new_rs)

    def slowest(rs):
        m = jnp.max(rs[0])
        for r in rs[1:]:
            m = jnp.maximum(m, jnp.max(r))
        return m

    accs, rs = block(qi, (jnp.zeros((tq, HEAD_DIM), F32),) * heads, (jnp.zeros((tq, 1), F32),) * heads, True)

    def cond(carry):
        j, _, _, r_max = carry
        return jnp.logical_and(j >= 0, r_max > LOG_F32_ZERO)

    def body(carry):
        j, accs, rs, _ = carry
        accs, rs = block(j, accs, rs, False)
        return j - 1, accs, rs, slowest(rs)

    _, accs, _, _ = lax.while_loop(cond, body, (qi - 1, accs, rs, slowest(rs)))
    for g in range(heads):
        o_ref[0, :, g * HEAD_DIM:(g + 1) * HEAD_DIM] = accs[g].astype(o_ref.dtype)


def _attn_call(qkv, d):
    n3h, b, s, _ = qkv.shape
    nh = n3h // 3
    tq = min(s, 256)
    heads = 4 if nh % 4 == 0 else 1
    ng = nh // heads
    return pl.pallas_call(
        functools.partial(_attn_kernel, tq=tq, heads=heads),
        grid=(b, ng, s // tq),
        in_specs=[pl.BlockSpec((heads, 1, tq, HEAD_DIM), lambda bi, h, qi: (h, bi, qi, 0)),
                  pl.BlockSpec((heads, 1, s, HEAD_DIM), lambda bi, h, qi: (ng + h, bi, 0, 0)),
                  pl.BlockSpec((heads, 1, s, HEAD_DIM), lambda bi, h, qi: (2 * ng + h, bi, 0, 0))],
        out_specs=pl.BlockSpec((1, tq, heads * HEAD_DIM), lambda bi, h, qi: (bi, qi, h)),
        out_shape=jax.ShapeDtypeStruct((b, s, d), BF16),
        compiler_params=_params("arbitrary", "arbitrary", "arbitrary"),
        name="stickbreak_attn",
    )(qkv, qkv, qkv)


def _oproj_kernel(o_ref, w_ref, x_ref, gt_ref, out_ref):
    out_ref[0] = x_ref[0] + gt_ref[0] * jnp.dot(o_ref[0], w_ref[...], preferred_element_type=F32)


def _oproj_call(o, w_o, x, gate):
    b, s, d = x.shape
    tm = min(s, 512)
    tn = min(d, 512)
    return pl.pallas_call(
        _oproj_kernel,
        grid=(b, s // tm, d // tn),
        in_specs=[pl.BlockSpec((1, tm, d), lambda bi, si, j: (bi, si, 0)),
                  pl.BlockSpec((d, tn), lambda bi, si, j: (0, j)),
                  pl.BlockSpec((1, tm, tn), lambda bi, si, j: (bi, si, j)),
                  pl.BlockSpec((1, 1, tn), lambda bi, si, j: (bi, 0, j))],
        out_specs=pl.BlockSpec((1, tm, tn), lambda bi, si, j: (bi, si, j)),
        out_shape=jax.ShapeDtypeStruct((b, s, d), F32),
        compiler_params=_params("arbitrary", "arbitrary", "arbitrary"),
        name="attn_out_proj",
    )(o, w_o.astype(BF16), x, gate)


def _split_bf16(a):
    hi = a.astype(BF16)
    return hi, (a - hi.astype(F32)).astype(BF16)


def _router_kernel(x_ref, sh_ref, sc_ref, g_ref, wr_ref, br_ref, h_ref, route_ref, idx_ref, cnt_ref,
                   *, ts, n_groups, n_exp):
    first = jnp.logical_and(pl.program_id(0) == 0, pl.program_id(1) == 0)

    @pl.when(first)
    def _():
        cnt_ref[...] = jnp.zeros_like(cnt_ref)

    h = _modulate(x_ref[0], g_ref[...], sh_ref[0], sc_ref[0])
    h_ref[0] = h

    h_hi, h_lo = _split_bf16(h)
    w_hi, w_lo = _split_bf16(wr_ref[...])
    logits = (jnp.dot(h_hi, w_hi, preferred_element_type=F32)
              + jnp.dot(h_hi, w_lo, preferred_element_type=F32)
              + jnp.dot(h_lo, w_hi, preferred_element_type=F32)) + br_ref[...]

    lane = lax.broadcasted_iota(I32, (ts, LANES), 1)
    neg = jnp.float32(-jnp.inf)

    def first_argmax(vals):
        m = jnp.max(vals, axis=-1, keepdims=True)
        idx = jnp.min(jnp.where(vals == m, lane, LANES), axis=-1, keepdims=True)
        return m, idx

    is_grp = lane < n_groups
    g_max, g_idx = first_argmax(jnp.where(is_grp, logits, neg))
    g_sum = jnp.sum(jnp.where(is_grp, jnp.exp(logits - g_max), 0.0), axis=-1, keepdims=True)
    p_grp = 1.0 / g_sum
    lo = n_groups + g_idx * n_exp
    in_grp = jnp.logical_and(lane >= lo, lane < lo + n_exp)
    sel = jnp.where(in_grp, logits, neg)
    v1, i1 = first_argmax(sel)
    v2, i2 = first_argmax(jnp.where(lane == i1, neg, sel))
    e21 = jnp.exp(v2 - v1)
    w1 = p_grp * (1.0 / (1.0 + e21))
    w2 = p_grp * (e21 / (1.0 + e21))

    oh1 = (lane == i1).astype(F32)
    oh2 = (lane == i2).astype(F32)
    both = oh1 + oh2
    earlier = (lax.broadcasted_iota(I32, (ts, ts), 1) < lax.broadcasted_iota(I32, (ts, ts), 0))
    ranks = jnp.dot(earlier.astype(BF16), both.astype(BF16), preferred_element_type=F32) + cnt_ref[...]
    r1 = jnp.sum(oh1 * ranks, axis=-1, keepdims=True)
    r2 = jnp.sum(oh2 * ranks, axis=-1, keepdims=True)
    cnt_ref[...] = cnt_ref[...] + jnp.sum(both, axis=0, keepdims=True)

    out = jnp.zeros((ts, LANES), F32)
    for k, val in enumerate(((i1 - n_groups).astype(F32), (i2 - n_groups).astype(F32), w1, w2, r1, r2)):
        out = jnp.where(lane == k, val, out)
    route_ref[0] = out
    idx_ref[0] = out.T[0:ROUTE_FIELDS, :].astype(I32)


def _router_call(x, shift, scale, norm_g, w_grp, b_grp, w_exp, b_exp):
    b, s, d = x.shape
    n_groups, n_exp = b_exp.shape
    n_logits = n_groups + n_groups * n_exp
    ts = min(s, 256)
    tiles = s // ts
    w_r = jnp.concatenate([w_grp, w_exp.reshape(d, n_groups * n_exp),
                           jnp.zeros((d, LANES - n_logits), F32)], axis=1)
    b_r = jnp.concatenate([b_grp, b_exp.reshape(-1), jnp.zeros((LANES - n_logits,), F32)]).reshape(1, LANES)
    vec = pl.BlockSpec((1, 1, d), lambda bi, si: (bi, 0, 0))
    return pl.pallas_call(
        functools.partial(_router_kernel, ts=ts, n_groups=n_groups, n_exp=n_exp),
        grid=(b, s // ts),
        in_specs=[pl.BlockSpec((1, ts, d), lambda bi, si: (bi, si, 0)), vec, vec,
                  pl.BlockSpec((1, d), lambda bi, si: (0, 0)),
                  pl.BlockSpec((d, LANES), lambda bi, si: (0, 0)),
                  pl.BlockSpec((1, LANES), lambda bi, si: (0, 0))],
        out_specs=[pl.BlockSpec((1, ts, d), lambda bi, si: (bi, si, 0)),
                   pl.BlockSpec((1, ts, LANES), lambda bi, si: (bi, si, 0)),
                   pl.BlockSpec((1, ROUTE_FIELDS, ts), lambda bi, si: (bi * tiles + si, 0, 0)),
                   pl.BlockSpec((1, LANES), lambda bi, si: (0, 0))],
        out_shape=[jax.ShapeDtypeStruct((b, s, d), F32),
                   jax.ShapeDtypeStruct((b, s, LANES), F32),
                   jax.ShapeDtypeStruct((b * tiles, ROUTE_FIELDS, ts), I32),
                   jax.ShapeDtypeStruct((1, LANES), F32)],
        compiler_params=_params("arbitrary", "arbitrary"),
        name="moe_router",
    )(x, shift, scale, norm_g.reshape(1, d), w_r, b_r)


def _row_gather_start(src_hbm, dst, sem, idx_ref, base):
    n_rows = dst.shape[0]

    def body(c, carry):
        for u in range(DMA_ISSUE_UNROLL):
            r = c * DMA_ISSUE_UNROLL + u
            pltpu.make_async_copy(src_hbm.at[pl.ds(idx_ref[base + r], 1)], dst.at[pl.ds(r, 1)], sem).start()
        return carry
    lax.fori_loop(0, n_rows // DMA_ISSUE_UNROLL, body, 0)


def _row_gather_wait(src_hbm, dst, sem):
    pltpu.make_async_copy(src_hbm.at[pl.ds(0, dst.shape[0])], dst, sem).wait()


def _expert_kernel(pos_ref, off_ref, cnt_ref, te_ref, nact_ref, h_hbm, wg_ref, wu_ref, wd_ref,
                   y_ref, xbuf, sem, tok_ref, wg_bf, wu_bf, wd_bf, *, tm, n_tok, n_experts):
    t = pl.program_id(0)
    n_active = nact_ref[0]

    @pl.when(t == 0)
    def _():
        def pad_expert(e, carry):
            first = off_ref[e] + cnt_ref[e]
            last = off_ref[e] + ((cnt_ref[e] + tm - 1) // tm) * tm

            def pad(p, c):
                tok_ref[p] = 0
                return c
            return lax.fori_loop(first, last, pad, carry)
        lax.fori_loop(0, n_experts, pad_expert, 0)

        def place(c, carry):
            toks = [c * DMA_ISSUE_UNROLL + u for u in range(DMA_ISSUE_UNROLL)]
            rows = [(pos_ref[i], pos_ref[n_tok + i]) for i in toks]
            for i, (a, b) in zip(toks, rows):
                tok_ref[a] = i
                tok_ref[b] = i
            return carry
        lax.fori_loop(0, n_tok // DMA_ISSUE_UNROLL, place, 0)
        _row_gather_start(h_hbm, xbuf.at[0], sem.at[0], tok_ref, 0)

    @pl.when(t + 1 < n_active)
    def _():
        nxt = (t + 1) % 2
        _row_gather_start(h_hbm, xbuf.at[nxt], sem.at[nxt], tok_ref, (t + 1) * tm)

    @pl.when(t < n_active)
    def _():
        slot = t % 2
        new_expert = jnp.logical_or(t == 0, te_ref[t] != te_ref[jnp.maximum(t - 1, 0)])

        @pl.when(new_expert)
        def _():
            wg_bf[...] = wg_ref[0].astype(BF16)
            wu_bf[...] = wu_ref[0].astype(BF16)
            wd_bf[...] = wd_ref[0].astype(BF16)

        _row_gather_wait(h_hbm, xbuf.at[slot], sem.at[slot])
        x = xbuf[slot].astype(BF16)
        a = jnp.dot(x, wg_bf[...], preferred_element_type=F32)
        u = jnp.dot(x, wu_bf[...], preferred_element_type=F32)
        hid = (a * (1.0 / (1.0 + jnp.exp(-a)))) * u
        y_ref[...] = jnp.dot(hid.astype(BF16), wd_bf[...], preferred_element_type=F32)

    @pl.when(t >= n_active)
    def _():
        y_ref[...] = jnp.zeros_like(y_ref)


def _expert_call(h2, pos, offset, cnt, tile_expert, n_active, w_gate, w_up, w_down, layer, tm):
    t, d = h2.shape
    n_all, _, f = w_gate.shape
    n_tiles = tile_expert.shape[0]
    experts_per_layer = n_all // layer[1]
    base = layer[0] * experts_per_layer

    def w_map(ti, pos, off, cnt, te, na):
        return (base + te[ti], 0, 0)

    grid_spec = pltpu.PrefetchScalarGridSpec(
        num_scalar_prefetch=5,
        grid=(n_tiles,),
        in_specs=[pl.BlockSpec(memory_space=pl.ANY),
                  pl.BlockSpec((1, d, f), w_map),
                  pl.BlockSpec((1, d, f), w_map),
                  pl.BlockSpec((1, f, d), w_map)],
        out_specs=pl.BlockSpec((tm, d), lambda ti, *_: (ti, 0)),
        scratch_shapes=[pltpu.VMEM((2, tm, d), F32),
                        pltpu.SemaphoreType.DMA((2,)),
                        pltpu.SMEM((n_tiles * tm,), I32),
                        pltpu.VMEM((d, f), BF16), pltpu.VMEM((d, f), BF16), pltpu.VMEM((f, d), BF16)],
    )
    return pl.pallas_call(
        functools.partial(_expert_kernel, tm=tm, n_tok=t, n_experts=experts_per_layer),
        grid_spec=grid_spec,
        out_shape=jax.ShapeDtypeStruct((n_tiles * tm, d), F32),
        compiler_params=_params("arbitrary"),
        name="moe_experts",
    )(pos, offset, cnt, tile_expert, n_active, h2, w_gate, w_up, w_down)


def _combine_kernel(pos_ref, x_ref, route_ref, gt_ref, y_hbm, o_ref, ybuf, sem, *, ts, tiles_per_seq):
    b = pl.program_id(0)
    s = pl.program_id(1)
    step = b * tiles_per_seq + s
    n_steps = pl.num_programs(0) * tiles_per_seq
    n_tok = n_steps * ts

    def start(tile, slot):
        _row_gather_start(y_hbm, ybuf.at[slot, 0], sem.at[slot], pos_ref, tile * ts)
        _row_gather_start(y_hbm, ybuf.at[slot, 1], sem.at[slot], pos_ref, n_tok + tile * ts)

    @pl.when(step == 0)
    def _():
        start(0, 0)

    @pl.when(step + 1 < n_steps)
    def _():
        start(step + 1, (step + 1) % 2)

    slot = step % 2
    _row_gather_wait(y_hbm, ybuf.at[slot, 0], sem.at[slot])
    _row_gather_wait(y_hbm, ybuf.at[slot, 1], sem.at[slot])
    route = route_ref[0]
    moe = route[:, 2:3] * ybuf[slot, 0] + route[:, 3:4] * ybuf[slot, 1]
    o_ref[0] = x_ref[0] + gt_ref[0] * moe


def _combine_call(x, route, gate, y, pos):
    b, s, d = x.shape
    ts = min(s, 256)
    tiles_per_seq = s // ts
    grid_spec = pltpu.PrefetchScalarGridSpec(
        num_scalar_prefetch=1,
        grid=(b, tiles_per_seq),
        in_specs=[pl.BlockSpec((1, ts, d), lambda bi, si, pos: (bi, si, 0)),
                  pl.BlockSpec((1, ts, LANES), lambda bi, si, pos: (bi, si, 0)),
                  pl.BlockSpec((1, 1, d), lambda bi, si, pos: (bi, 0, 0)),
                  pl.BlockSpec(memory_space=pl.ANY)],
        out_specs=pl.BlockSpec((1, ts, d), lambda bi, si, pos: (bi, si, 0)),
        scratch_shapes=[pltpu.VMEM((2, 2, ts, d), F32), pltpu.SemaphoreType.DMA((2,))],
    )
    return pl.pallas_call(
        functools.partial(_combine_kernel, ts=ts, tiles_per_seq=tiles_per_seq),
        grid_spec=grid_spec,
        out_shape=jax.ShapeDtypeStruct((b, s, d), F32),
        compiler_params=_params("arbitrary", "arbitrary"),
        name="moe_combine",
    )(pos, x, route, gate, y)


def _moe(x, shift, scale, gate, norm_g, w_grp, b_grp, w_exp, b_exp, w_gate, w_up, w_down, layer):
    b, s, d = x.shape
    t = b * s
    n_groups, n_exp = b_exp.shape
    n_experts = n_groups * n_exp
    tm = min(t, 256)

    h2, route, idx, counts = _router_call(x, shift, scale, norm_g, w_grp, b_grp, w_exp, b_exp)

    cnt = counts[0, n_groups:n_groups + n_experts].astype(I32)
    tiles_e = (cnt + tm - 1) // tm
    tiles_end = jnp.cumsum(tiles_e)
    offset = (tiles_end - tiles_e) * tm
    n_tiles = (2 * t + n_experts * (tm - 1) + tm - 1) // tm
    n_active = tiles_end[-1:]
    tile_ids = jnp.minimum(jnp.arange(n_tiles, dtype=I32), n_active[0] - 1)
    tile_expert = jnp.sum((tiles_end[None, :] <= tile_ids[:, None]).astype(I32), axis=1)
    fields = idx.transpose(1, 0, 2).reshape(ROUTE_FIELDS, t)
    expert = jnp.concatenate([fields[0], fields[1]])
    rank = jnp.concatenate([fields[4], fields[5]])
    is_e = expert[None, :] == jnp.arange(n_experts, dtype=I32)[:, None]
    pos = jnp.sum(jnp.where(is_e, offset[:, None], 0), axis=0) + rank

    y = _expert_call(h2.reshape(t, d), pos, offset, cnt, tile_expert, n_active,
                     w_gate, w_up, w_down, layer, tm)
    return _combine_call(x, route, gate, y, pos)


def kernel(x, c, ada_w, ada_b, norm1_g, norm2_g, pool_w, pool_scale, w_qkv, w_o, q_norm_g, k_norm_g,
           w_grp, b_grp, w_exp, b_exp, w_gate, w_up, w_down):
    depth = ada_w.shape[0]
    b, s, d = x.shape
    n_mixers = 2
    mod = _ada_call(c, ada_w, ada_b).reshape(depth, b, 6, 1, d)
    wshape = w_gate.shape
    w_gate_s = w_gate.reshape((-1,) + wshape[3:])
    w_up_s = w_up.reshape((-1,) + wshape[3:])
    w_down_s = w_down.reshape((-1,) + w_down.shape[3:])
    for i in range(depth):
        shift1, scale1, gate1, shift2, scale2, gate2 = (mod[i, :, k] for k in range(6))
        j = i // n_mixers
        if i % n_mixers == 0:
            x = _pool_call(x, shift1, scale1, gate1, norm1_g[i], pool_w[j], pool_scale[j])
        else:
            qkv = _qkv_call(x, shift1, scale1, norm1_g[i], w_qkv[j], q_norm_g[j], k_norm_g[j])
            o = _attn_call(qkv, d)
            x = _oproj_call(o, w_o[j], x, gate1)
        x = _moe(x, shift2, scale2, gate2, norm2_g[i], w_grp[i], b_grp[i], w_exp[i], b_exp[i],
                 w_gate_s, w_up_s, w_down_s, (i, depth))
    return x
```

```python
import functools

import jax
import jax.numpy as jnp
from jax import lax
from jax.experimental import pallas as pl
from jax.experimental.pallas import tpu as pltpu

F32 = jnp.float32
BF16 = jnp.bfloat16
I32 = jnp.int32

NORM_EPS = 1e-6
POOL_WINDOWS = (2, 4, 8, 16)
MAX_WINDOW = max(POOL_WINDOWS)
HEAD_DIM = 128
LANES = 128
ROUTE_FIELDS = 8
DMA_ISSUE_UNROLL = 8
VMEM_LIMIT_BYTES = 56 * 1024 * 1024
LOG_F32_ZERO = -110.0


def _params(*sem):
    return pltpu.CompilerParams(dimension_semantics=sem, vmem_limit_bytes=VMEM_LIMIT_BYTES)


def _modulate(x, g, shift, scale):
    y = x * lax.rsqrt(jnp.mean(x * x, axis=-1, keepdims=True) + NORM_EPS)
    return (y * g) * (1.0 + scale) + shift


def _ada_kernel(c_ref, w_ref, b_ref, o_ref):
    c = c_ref[...]
    c_act = c * (1.0 / (1.0 + jnp.exp(-c)))
    o_ref[0] = jnp.dot(c_act.astype(BF16), w_ref[0].astype(BF16),
                       preferred_element_type=F32) + b_ref[0]


def _ada_call(c, ada_w, ada_b):
    depth, d, n = ada_w.shape
    b = c.shape[0]
    tn = min(n, 1024)
    return pl.pallas_call(
        _ada_kernel,
        grid=(depth, n // tn),
        in_specs=[pl.BlockSpec((b, d), lambda l, j: (0, 0)),
                  pl.BlockSpec((1, d, tn), lambda l, j: (l, 0, j)),
                  pl.BlockSpec((1, 1, tn), lambda l, j: (l, 0, j))],
        out_specs=pl.BlockSpec((1, b, tn), lambda l, j: (l, 0, j)),
        out_shape=jax.ShapeDtypeStruct((depth, b, n), F32),
        compiler_params=_params("arbitrary", "arbitrary"),
        name="ada",
    )(c, ada_w, ada_b.reshape(depth, 1, n))


def _pool_kernel(x_ref, sh_ref, sc_ref, gt_ref, g_ref, pw_ref, ps_ref, o_ref, hext_ref, *, ts, dg):
    s = pl.program_id(1)
    x = x_ref[0]
    h = _modulate(x, g_ref[...], sh_ref[0], sc_ref[0])

    @pl.when(s == 0)
    def _():
        hext_ref[0:MAX_WINDOW, :] = jnp.zeros((MAX_WINDOW, h.shape[1]), F32)

    @pl.when(s > 0)
    def _():
        hext_ref[0:MAX_WINDOW, :] = hext_ref[ts:ts + MAX_WINDOW, :]

    hext_ref[MAX_WINDOW:MAX_WINDOW + ts, :] = h
    pos = s * ts + lax.broadcasted_iota(I32, (ts, dg), 0)
    for gi, w in enumerate(POOL_WINDOWS):
        c0 = gi * dg
        acc = h[:, c0:c0 + dg]
        for k in range(1, w):
            acc = acc + hext_ref[MAX_WINDOW - k:MAX_WINDOW - k + ts, c0:c0 + dg]
        count = jnp.minimum(pos + 1, w).astype(F32)
        pooled = acc / count - h[:, c0:c0 + dg]
        mixed = jnp.dot(pooled.astype(BF16), pw_ref[gi], preferred_element_type=F32)
        o_ref[0, :, c0:c0 + dg] = (x[:, c0:c0 + dg]
                                   + gt_ref[0][:, c0:c0 + dg] * (mixed * ps_ref[:, c0:c0 + dg]))


def _pool_call(x, shift, scale, gate, norm_g, pool_w, pool_scale):
    b, s, d = x.shape
    ng, dg, _ = pool_w.shape
    ts = min(s, 256)
    vec = pl.BlockSpec((1, 1, d), lambda bi, si: (bi, 0, 0))
    row = pl.BlockSpec((1, d), lambda bi, si: (0, 0))
    return pl.pallas_call(
        functools.partial(_pool_kernel, ts=ts, dg=dg),
        grid=(b, s // ts),
        in_specs=[pl.BlockSpec((1, ts, d), lambda bi, si: (bi, si, 0)), vec, vec, vec, row,
                  pl.BlockSpec((ng, dg, dg), lambda bi, si: (0, 0, 0)), row],
        out_specs=pl.BlockSpec((1, ts, d), lambda bi, si: (bi, si, 0)),
        out_shape=jax.ShapeDtypeStruct((b, s, d), F32),
        scratch_shapes=[pltpu.VMEM((ts + MAX_WINDOW, d), F32)],
        compiler_params=_params("arbitrary", "arbitrary"),
        name="pool_mixer",
    )(x, shift, scale, gate, norm_g.reshape(1, d), pool_w.astype(BF16), pool_scale.reshape(1, d))


def _qkv_kernel(x_ref, sh_ref, sc_ref, g_ref, w_ref, ng_ref, o_ref, h_ref, res_ref, *, n_norm_tiles, heads):
    j = pl.program_id(2)
    tm = h_ref.shape[0]
    rows = min(tm, 256)
    sub = min(rows, 64)

    @pl.when(j == 0)
    def _():
        h_ref[...] = _modulate(x_ref[0], g_ref[...], sh_ref[0], sc_ref[0]).astype(BF16)

    is_norm = j < n_norm_tiles
    g = ng_ref[0]
    for r0 in range(0, tm, rows):
        res_ref[r0:r0 + rows, :] = jnp.dot(h_ref[r0:r0 + rows, :], w_ref[...], preferred_element_type=F32)
        for hh in range(heads):
            for r1 in range(r0, r0 + rows, sub):
                blk = res_ref[r1:r1 + sub, hh * HEAD_DIM:(hh + 1) * HEAD_DIM]
                inv = lax.rsqrt(jnp.mean(blk * blk, axis=-1, keepdims=True) + NORM_EPS)
                y = jnp.where(is_norm, (blk * inv) * g, blk)
                o_ref[hh, 0, r1:r1 + sub, :] = y.astype(BF16)


def _qkv_call(x, shift, scale, norm_g, w_qkv, q_norm_g, k_norm_g):
    b, s, d = x.shape
    n = w_qkv.shape[1]
    tm = min(s, 512)
    tn = min(d, 1024)
    heads = tn // HEAD_DIM
    tiles_per_part = d // tn
    qk_gain = jnp.stack([q_norm_g * (HEAD_DIM ** -0.5), k_norm_g, jnp.ones_like(k_norm_g)])
    vec = pl.BlockSpec((1, 1, d), lambda bi, si, j: (bi, 0, 0))
    return pl.pallas_call(
        functools.partial(_qkv_kernel, n_norm_tiles=2 * tiles_per_part, heads=heads),
        grid=(b, s // tm, n // tn),
        in_specs=[pl.BlockSpec((1, tm, d), lambda bi, si, j: (bi, si, 0)), vec, vec,
                  pl.BlockSpec((1, d), lambda bi, si, j: (0, 0)),
                  pl.BlockSpec((d, tn), lambda bi, si, j: (0, j)),
                  pl.BlockSpec((1, 1, HEAD_DIM), lambda bi, si, j: (j // tiles_per_part, 0, 0))],
        out_specs=pl.BlockSpec((heads, 1, tm, HEAD_DIM), lambda bi, si, j: (j, bi, si, 0)),
        out_shape=jax.ShapeDtypeStruct((n // HEAD_DIM, b, s, HEAD_DIM), BF16),
        scratch_shapes=[pltpu.VMEM((tm, d), BF16), pltpu.VMEM((tm, tn), F32)],
        compiler_params=_params("arbitrary", "arbitrary", "arbitrary"),
        name="qkv_proj",
    )(x, shift, scale, norm_g.reshape(1, d), w_qkv.astype(BF16), qk_gain.reshape(3, 1, HEAD_DIM))


def _attn_kernel(q_ref, k_ref, v_ref, o_ref, *, tq, heads):
    qi = pl.program_id(2)
    row = lax.broadcasted_iota(I32, (tq, tq), 0)
    col = lax.broadcasted_iota(I32, (tq, tq), 1)
    suffix_sum = (row >= col).astype(BF16)
    causal = col < row

    def block(j, accs, rs, masked):
        start = pl.multiple_of(j * tq, tq)
        new_accs, new_rs = [], []
        for g in range(heads):
            k = k_ref[g, 0, pl.ds(start, tq), :]
            v = v_ref[g, 0, pl.ds(start, tq), :]
            z = lax.dot_general(q_ref[g, 0], k, (((1,), (1,)), ((), ())), preferred_element_type=F32)
            log_fail = -(jnp.maximum(z, 0.0) + jnp.log(1.0 + jnp.exp(-jnp.abs(z))))
            if masked:
                log_fail = jnp.where(causal, log_fail, 0.0)
            cum = jnp.dot(log_fail.astype(BF16), suffix_sum, preferred_element_type=F32)
            a = jnp.exp(z + cum + rs[g])
            if masked:
                a = jnp.where(causal, a, 0.0)
            new_accs.append(accs[g] + jnp.dot(a.astype(BF16), v, preferred_element_type=F32))
            new_rs.append(rs[g] + cum[:, 0:1])
        return tuple(new_accs), tuple(new_rs)

    def slowest(rs):
        m = jnp.max(rs[0])
        for r in rs[1:]:
            m = jnp.maximum(m, jnp.max(r))
        return m

    accs, rs = block(qi, (jnp.zeros((tq, HEAD_DIM), F32),) * heads, (jnp.zeros((tq, 1), F32),) * heads, True)

    def cond(carry):
        j, _, _, r_max = carry
        return jnp.logical_and(j >= 0, r_max > LOG_F32_ZERO)

    def body(carry):
        j, accs, rs, _ = carry
        accs, rs = block(j, accs, rs, False)
        return j - 1, accs, rs, slowest(rs)

    _, accs, _, _ = lax.while_loop(cond, body, (qi - 1, accs, rs, slowest(rs)))
    for g in range(heads):
        o_ref[0, :, g * HEAD_DIM:(g + 1) * HEAD_DIM] = accs[g].astype(o_ref.dtype)


def _attn_call(qkv, d):
    n3h, b, s, _ = qkv.shape
    nh = n3h // 3
    tq = min(s, 256)
    heads = 4 if nh % 4 == 0 else 1
    ng = nh // heads
    return pl.pallas_call(
        functools.partial(_attn_kernel, tq=tq, heads=heads),
        grid=(b, ng, s // tq),
        in_specs=[pl.BlockSpec((heads, 1, tq, HEAD_DIM), lambda bi, h, qi: (h, bi, qi, 0)),
                  pl.BlockSpec((heads, 1, s, HEAD_DIM), lambda bi, h, qi: (ng + h, bi, 0, 0)),
                  pl.BlockSpec((heads, 1, s, HEAD_DIM), lambda bi, h, qi: (2 * ng + h, bi, 0, 0))],
        out_specs=pl.BlockSpec((1, tq, heads * HEAD_DIM), lambda bi, h, qi: (bi, qi, h)),
        out_shape=jax.ShapeDtypeStruct((b, s, d), BF16),
        compiler_params=_params("arbitrary", "arbitrary", "arbitrary"),
        name="stickbreak_attn",
    )(qkv, qkv, qkv)


def _oproj_kernel(o_ref, w_ref, x_ref, gt_ref, out_ref):
    tm, d = o_ref.shape[1], w_ref.shape[1]
    rows, cols = min(tm, 256), min(d, 512)
    for r0 in range(0, tm, rows):
        for c0 in range(0, d, cols):
            res = jnp.dot(o_ref[0, r0:r0 + rows, :], w_ref[:, c0:c0 + cols], preferred_element_type=F32)
            out_ref[0, r0:r0 + rows, c0:c0 + cols] = (x_ref[0, r0:r0 + rows, c0:c0 + cols]
                                                      + gt_ref[0][:, c0:c0 + cols] * res)


def _oproj_call(o, w_o, x, gate):
    b, s, d = x.shape
    tm = min(s, 512)
    return pl.pallas_call(
        _oproj_kernel,
        grid=(b, s // tm),
        in_specs=[pl.BlockSpec((1, tm, d), lambda bi, si: (bi, si, 0)),
                  pl.BlockSpec((d, d), lambda bi, si: (0, 0)),
                  pl.BlockSpec((1, tm, d), lambda bi, si: (bi, si, 0)),
                  pl.BlockSpec((1, 1, d), lambda bi, si: (bi, 0, 0))],
        out_specs=pl.BlockSpec((1, tm, d), lambda bi, si: (bi, si, 0)),
        out_shape=jax.ShapeDtypeStruct((b, s, d), F32),
        compiler_params=_params("arbitrary", "arbitrary"),
        name="attn_out_proj",
    )(o, w_o.astype(BF16), x, gate)


def _split_bf16(a):
    hi = a.astype(BF16)
    return hi, (a - hi.astype(F32)).astype(BF16)


def _router_kernel(x_ref, sh_ref, sc_ref, g_ref, wr_ref, br_ref, h_ref, route_ref, idx_ref, cnt_ref,
                   *, ts, n_groups, n_exp):
    first = jnp.logical_and(pl.program_id(0) == 0, pl.program_id(1) == 0)

    @pl.when(first)
    def _():
        cnt_ref[...] = jnp.zeros_like(cnt_ref)

    h = _modulate(x_ref[0], g_ref[...], sh_ref[0], sc_ref[0])
    h_ref[0] = h

    h_hi, h_lo = _split_bf16(h)
    w_hi, w_lo = _split_bf16(wr_ref[...])
    logits = (jnp.dot(h_hi, w_hi, preferred_element_type=F32)
              + jnp.dot(h_hi, w_lo, preferred_element_type=F32)
              + jnp.dot(h_lo, w_hi, preferred_element_type=F32)) + br_ref[...]

    lane = lax.broadcasted_iota(I32, (ts, LANES), 1)
    neg = jnp.float32(-jnp.inf)

    def first_argmax(vals):
        m = jnp.max(vals, axis=-1, keepdims=True)
        idx = jnp.min(jnp.where(vals == m, lane, LANES), axis=-1, keepdims=True)
        return m, idx

    is_grp = lane < n_groups
    g_max, g_idx = first_argmax(jnp.where(is_grp, logits, neg))
    g_sum = jnp.sum(jnp.where(is_grp, jnp.exp(logits - g_max), 0.0), axis=-1, keepdims=True)
    p_grp = 1.0 / g_sum
    lo = n_groups + g_idx * n_exp
    in_grp = jnp.logical_and(lane >= lo, lane < lo + n_exp)
    sel = jnp.where(in_grp, logits, neg)
    v1, i1 = first_argmax(sel)
    v2, i2 = first_argmax(jnp.where(lane == i1, neg, sel))
    e21 = jnp.exp(v2 - v1)
    w1 = p_grp * (1.0 / (1.0 + e21))
    w2 = p_grp * (e21 / (1.0 + e21))

    oh1 = (lane == i1).astype(F32)
    oh2 = (lane == i2).astype(F32)
    both = oh1 + oh2
    earlier = (lax.broadcasted_iota(I32, (ts, ts), 1) < lax.broadcasted_iota(I32, (ts, ts), 0))
    ranks = jnp.dot(earlier.astype(BF16), both.astype(BF16), preferred_element_type=F32) + cnt_ref[...]
    r1 = jnp.sum(oh1 * ranks, axis=-1, keepdims=True)
    r2 = jnp.sum(oh2 * ranks, axis=-1, keepdims=True)
    cnt_ref[...] = cnt_ref[...] + jnp.sum(both, axis=0, keepdims=True)

    out = jnp.zeros((ts, LANES), F32)
    for k, val in enumerate(((i1 - n_groups).astype(F32), (i2 - n_groups).astype(F32), w1, w2, r1, r2)):
        out = jnp.where(lane == k, val, out)
    route_ref[0] = out
    idx_ref[0] = out.T[0:ROUTE_FIELDS, :].astype(I32)


def _router_call(x, shift, scale, norm_g, w_grp, b_grp, w_exp, b_exp):
    b, s, d = x.shape
    n_groups, n_exp = b_exp.shape
    n_logits = n_groups + n_groups * n_exp
    ts = min(s, 256)
    tiles = s // ts
    w_r = jnp.concatenate([w_grp, w_exp.reshape(d, n_groups * n_exp),
                           jnp.zeros((d, LANES - n_logits), F32)], axis=1)
    b_r = jnp.concatenate([b_grp, b_exp.reshape(-1), jnp.zeros((LANES - n_logits,), F32)]).reshape(1, LANES)
    vec = pl.BlockSpec((1, 1, d), lambda bi, si: (bi, 0, 0))
    return pl.pallas_call(
        functools.partial(_router_kernel, ts=ts, n_groups=n_groups, n_exp=n_exp),
        grid=(b, s // ts),
        in_specs=[pl.BlockSpec((1, ts, d), lambda bi, si: (bi, si, 0)), vec, vec,
                  pl.BlockSpec((1, d), lambda bi, si: (0, 0)),
                  pl.BlockSpec((d, LANES), lambda bi, si: (0, 0)),
                  pl.BlockSpec((1, LANES), lambda bi, si: (0, 0))],
        out_specs=[pl.BlockSpec((1, ts, d), lambda bi, si: (bi, si, 0)),
                   pl.BlockSpec((1, ts, LANES), lambda bi, si: (bi, si, 0)),
                   pl.BlockSpec((1, ROUTE_FIELDS, ts), lambda bi, si: (bi * tiles + si, 0, 0)),
                   pl.BlockSpec((1, LANES), lambda bi, si: (0, 0))],
        out_shape=[jax.ShapeDtypeStruct((b, s, d), F32),
                   jax.ShapeDtypeStruct((b, s, LANES), F32),
                   jax.ShapeDtypeStruct((b * tiles, ROUTE_FIELDS, ts), I32),
                   jax.ShapeDtypeStruct((1, LANES), F32)],
        compiler_params=_params("arbitrary", "arbitrary"),
        name="moe_router",
    )(x, shift, scale, norm_g.reshape(1, d), w_r, b_r)


def _row_gather_start(src_hbm, dst, sem, idx_ref, base):
    n_rows = dst.shape[0]

    def body(c, carry):
        for u in range(DMA_ISSUE_UNROLL):
            r = c * DMA_ISSUE_UNROLL + u
            pltpu.make_async_copy(src_hbm.at[pl.ds(idx_ref[base + r], 1)], dst.at[pl.ds(r, 1)], sem).start()
        return carry
    lax.fori_loop(0, n_rows // DMA_ISSUE_UNROLL, body, 0)


def _row_gather_wait(src_hbm, dst, sem):
    pltpu.make_async_copy(src_hbm.at[pl.ds(0, dst.shape[0])], dst, sem).wait()


def _expert_kernel(pos_ref, off_ref, cnt_ref, nact_ref, h_hbm, wg_ref, wu_ref, wd_ref, y_hbm,
                   xbuf, ybuf, gsem, ysem, tok_ref, wg_bf, wu_bf, wd_bf, *, tm, n_tok, n_experts, n_tiles):
    expert = pl.program_id(0)
    n_active = nact_ref[0]
    tiles_e = (cnt_ref[expert] + tm - 1) // tm
    first_tile = off_ref[expert] // tm

    def y_write(slot, g):
        return pltpu.make_async_copy(ybuf.at[slot], y_hbm.at[pl.ds(g * tm, tm)], ysem.at[slot])

    @pl.when(expert == 0)
    def _():
        def pad_expert(e, carry):
            first = off_ref[e] + cnt_ref[e]
            last = off_ref[e] + ((cnt_ref[e] + tm - 1) // tm) * tm

            def pad(p, c):
                tok_ref[p] = 0
                return c
            return lax.fori_loop(first, last, pad, carry)
        lax.fori_loop(0, n_experts, pad_expert, 0)

        def place(c, carry):
            toks = [c * DMA_ISSUE_UNROLL + u for u in range(DMA_ISSUE_UNROLL)]
            rows = [(pos_ref[i], pos_ref[n_tok + i]) for i in toks]
            for i, (a, b) in zip(toks, rows):
                tok_ref[a] = i
                tok_ref[b] = i
            return carry
        lax.fori_loop(0, n_tok // DMA_ISSUE_UNROLL, place, 0)
        _row_gather_start(h_hbm, xbuf.at[0], gsem.at[0], tok_ref, 0)

    @pl.when(tiles_e > 0)
    def _():
        wg_bf[...] = wg_ref[0].astype(BF16)
        wu_bf[...] = wu_ref[0].astype(BF16)
        wd_bf[...] = wd_ref[0].astype(BF16)

    def tile_body(i, carry):
        g = first_tile + i
        slot = g % 2

        @pl.when(g + 1 < n_active)
        def _():
            _row_gather_start(h_hbm, xbuf.at[1 - slot], gsem.at[1 - slot], tok_ref, (g + 1) * tm)

        _row_gather_wait(h_hbm, xbuf.at[slot], gsem.at[slot])
        x = xbuf[slot].astype(BF16)
        a = jnp.dot(x, wg_bf[...], preferred_element_type=F32)
        u = jnp.dot(x, wu_bf[...], preferred_element_type=F32)
        hid = (a * (1.0 / (1.0 + jnp.exp(-a)))) * u

        @pl.when(g >= 2)
        def _():
            y_write(slot, g).wait()

        ybuf[slot] = jnp.dot(hid.astype(BF16), wd_bf[...], preferred_element_type=F32)
        y_write(slot, g).start()
        return carry

    lax.fori_loop(0, tiles_e, tile_body, 0)

    @pl.when(expert == n_experts - 1)
    def _():
        @pl.when(n_active >= 2)
        def _():
            y_write(n_active % 2, 0).wait()

        y_write((n_active + 1) % 2, 0).wait()

        xbuf[0] = jnp.zeros(xbuf.shape[1:], F32)

        def zero_tile(g, carry):
            pltpu.make_async_copy(xbuf.at[0], y_hbm.at[pl.ds(g * tm, tm)], gsem.at[0]).start()
            return carry

        def zero_wait(g, carry):
            pltpu.make_async_copy(xbuf.at[0], y_hbm.at[pl.ds(0, tm)], gsem.at[0]).wait()
            return carry
        lax.fori_loop(n_active, n_tiles, zero_tile, 0)
        lax.fori_loop(n_active, n_tiles, zero_wait, 0)


def _expert_call(h2, pos, offset, cnt, n_active, n_tiles, w_gate, w_up, w_down, layer, tm):
    t, d = h2.shape
    n_all, _, f = w_gate.shape
    experts_per_layer = n_all // layer[1]
    base = layer[0] * experts_per_layer

    def w_map(e, *_):
        return (base + e, 0, 0)

    grid_spec = pltpu.PrefetchScalarGridSpec(
        num_scalar_prefetch=4,
        grid=(experts_per_layer,),
        in_specs=[pl.BlockSpec(memory_space=pl.ANY),
                  pl.BlockSpec((1, d, f), w_map),
                  pl.BlockSpec((1, d, f), w_map),
                  pl.BlockSpec((1, f, d), w_map)],
        out_specs=pl.BlockSpec(memory_space=pl.ANY),
        scratch_shapes=[pltpu.VMEM((2, tm, d), F32),
                        pltpu.VMEM((2, tm, d), F32),
                        pltpu.SemaphoreType.DMA((2,)),
                        pltpu.SemaphoreType.DMA((2,)),
                        pltpu.SMEM((n_tiles * tm,), I32),
                        pltpu.VMEM((d, f), BF16), pltpu.VMEM((d, f), BF16), pltpu.VMEM((f, d), BF16)],
    )
    return pl.pallas_call(
        functools.partial(_expert_kernel, tm=tm, n_tok=t, n_experts=experts_per_layer, n_tiles=n_tiles),
        grid_spec=grid_spec,
        out_shape=jax.ShapeDtypeStruct((n_tiles * tm, d), F32),
        compiler_params=_params("arbitrary"),
        name="moe_experts",
    )(pos, offset, cnt, n_active, h2, w_gate, w_up, w_down)


def _combine_kernel(pos_ref, x_ref, route_ref, gt_ref, y_hbm, o_ref, ybuf, sem, *, ts, tiles_per_seq):
    b = pl.program_id(0)
    s = pl.program_id(1)
    step = b * tiles_per_seq + s
    n_steps = pl.num_programs(0) * tiles_per_seq
    n_tok = n_steps * ts

    def start(tile, slot):
        _row_gather_start(y_hbm, ybuf.at[slot, 0], sem.at[slot], pos_ref, tile * ts)
        _row_gather_start(y_hbm, ybuf.at[slot, 1], sem.at[slot], pos_ref, n_tok + tile * ts)

    @pl.when(step == 0)
    def _():
        start(0, 0)

    @pl.when(step + 1 < n_steps)
    def _():
        start(step + 1, (step + 1) % 2)

    slot = step % 2
    _row_gather_wait(y_hbm, ybuf.at[slot, 0], sem.at[slot])
    _row_gather_wait(y_hbm, ybuf.at[slot, 1], sem.at[slot])
    route = route_ref[0]
    moe = route[:, 2:3] * ybuf[slot, 0] + route[:, 3:4] * ybuf[slot, 1]
    o_ref[0] = x_ref[0] + gt_ref[0] * moe


def _combine_call(x, route, gate, y, pos):
    b, s, d = x.shape
    ts = min(s, 256)
    tiles_per_seq = s // ts
    grid_spec = pltpu.PrefetchScalarGridSpec(
        num_scalar_prefetch=1,
        grid=(b, tiles_per_seq),
        in_specs=[pl.BlockSpec((1, ts, d), lambda bi, si, pos: (bi, si, 0)),
                  pl.BlockSpec((1, ts, LANES), lambda bi, si, pos: (bi, si, 0)),
                  pl.BlockSpec((1, 1, d), lambda bi, si, pos: (bi, 0, 0)),
                  pl.BlockSpec(memory_space=pl.ANY)],
        out_specs=pl.BlockSpec((1, ts, d), lambda bi, si, pos: (bi, si, 0)),
        scratch_shapes=[pltpu.VMEM((2, 2, ts, d), F32), pltpu.SemaphoreType.DMA((2,))],
    )
    return pl.pallas_call(
        functools.partial(_combine_kernel, ts=ts, tiles_per_seq=tiles_per_seq),
        grid_spec=grid_spec,
        out_shape=jax.ShapeDtypeStruct((b, s, d), F32),
        compiler_params=_params("arbitrary", "arbitrary"),
        name="moe_combine",
    )(pos, x, route, gate, y)


def _moe(x, shift, scale, gate, norm_g, w_grp, b_grp, w_exp, b_exp, w_gate, w_up, w_down, layer):
    b, s, d = x.shape
    t = b * s
    n_groups, n_exp = b_exp.shape
    n_experts = n_groups * n_exp
    tm = min(t, 256)

    h2, route, idx, counts = _router_call(x, shift, scale, norm_g, w_grp, b_grp, w_exp, b_exp)

    cnt = counts[0, n_groups:n_groups + n_experts].astype(I32)
    tiles_e = (cnt + tm - 1) // tm
    tiles_end = jnp.cumsum(tiles_e)
    offset = (tiles_end - tiles_e) * tm
    n_tiles = (2 * t + n_experts * (tm - 1) + tm - 1) // tm
    n_active = tiles_end[-1:]
    fields = idx.transpose(1, 0, 2).reshape(ROUTE_FIELDS, t)
    expert = jnp.concatenate([fields[0], fields[1]])
    rank = jnp.concatenate([fields[4], fields[5]])
    is_e = expert[None, :] == jnp.arange(n_experts, dtype=I32)[:, None]
    pos = jnp.sum(jnp.where(is_e, offset[:, None], 0), axis=0) + rank

    y = _expert_call(h2.reshape(t, d), pos, offset, cnt, n_active, n_tiles,
                     w_gate, w_up, w_down, layer, tm)
    return _combine_call(x, route, gate, y, pos)


def kernel(x, c, ada_w, ada_b, norm1_g, norm2_g, pool_w, pool_scale, w_qkv, w_o, q_norm_g, k_norm_g,
           w_grp, b_grp, w_exp, b_exp, w_gate, w_up, w_down):
    depth = ada_w.shape[0]
    b, s, d = x.shape
    n_mixers = 2
    mod = _ada_call(c, ada_w, ada_b).reshape(depth, b, 6, 1, d)
    wshape = w_gate.shape
    w_gate_s = w_gate.reshape((-1,) + wshape[3:])
    w_up_s = w_up.reshape((-1,) + wshape[3:])
    w_down_s = w_down.reshape((-1,) + w_down.shape[3:])
    for i in range(depth):
        shift1, scale1, gate1, shift2, scale2, gate2 = (mod[i, :, k] for k in range(6))
        j = i // n_mixers
        if i % n_mixers == 0:
            x = _pool_call(x, shift1, scale1, gate1, norm1_g[i], pool_w[j], pool_scale[j])
        else:
            qkv = _qkv_call(x, shift1, scale1, norm1_g[i], w_qkv[j], q_norm_g[j], k_norm_g[j])
            o = _attn_call(qkv, d)
            x = _oproj_call(o, w_o[j], x, gate1)
        x = _moe(x, shift2, scale2, gate2, norm2_g[i], w_grp[i], b_grp[i], w_exp[i], b_exp[i],
                 w_gate_s, w_up_s, w_down_s, (i, depth))
    return x
```

```python
import functools

import jax
import jax.numpy as jnp
from jax import lax
from jax.experimental import pallas as pl
from jax.experimental.pallas import tpu as pltpu

F32 = jnp.float32
BF16 = jnp.bfloat16
I32 = jnp.int32

NORM_EPS = 1e-6
POOL_WINDOWS = (2, 4, 8, 16)
MAX_WINDOW = max(POOL_WINDOWS)
HEAD_DIM = 128
LANES = 128
ROUTE_FIELDS = 8
DMA_ISSUE_UNROLL = 8
GATHER_SLOTS = 3
VMEM_LIMIT_BYTES = 56 * 1024 * 1024
LOG_F32_ZERO = -110.0


def _params(*sem):
    return pltpu.CompilerParams(dimension_semantics=sem, vmem_limit_bytes=VMEM_LIMIT_BYTES)


def _modulate(x, g, shift, scale):
    y = x * lax.rsqrt(jnp.mean(x * x, axis=-1, keepdims=True) + NORM_EPS)
    return (y * g) * (1.0 + scale) + shift


def _ada_kernel(c_ref, w_ref, b_ref, o_ref):
    c = c_ref[...]
    c_act = c * (1.0 / (1.0 + jnp.exp(-c)))
    o_ref[0] = jnp.dot(c_act.astype(BF16), w_ref[0].astype(BF16),
                       preferred_element_type=F32) + b_ref[0]


def _ada_call(c, ada_w, ada_b):
    depth, d, n = ada_w.shape
    b = c.shape[0]
    tn = min(n, 1024)
    return pl.pallas_call(
        _ada_kernel,
        grid=(depth, n // tn),
        in_specs=[pl.BlockSpec((b, d), lambda l, j: (0, 0)),
                  pl.BlockSpec((1, d, tn), lambda l, j: (l, 0, j)),
                  pl.BlockSpec((1, 1, tn), lambda l, j: (l, 0, j))],
        out_specs=pl.BlockSpec((1, b, tn), lambda l, j: (l, 0, j)),
        out_shape=jax.ShapeDtypeStruct((depth, b, n), F32),
        compiler_params=_params("arbitrary", "arbitrary"),
        name="ada",
    )(c, ada_w, ada_b.reshape(depth, 1, n))


def _pool_kernel(x_ref, sh_ref, sc_ref, gt_ref, g_ref, pw_ref, ps_ref, o_ref, hext_ref, *, ts, dg):
    s = pl.program_id(1)
    x = x_ref[0]
    h = _modulate(x, g_ref[...], sh_ref[0], sc_ref[0])

    @pl.when(s == 0)
    def _():
        hext_ref[0:MAX_WINDOW, :] = jnp.zeros((MAX_WINDOW, h.shape[1]), F32)

    @pl.when(s > 0)
    def _():
        hext_ref[0:MAX_WINDOW, :] = hext_ref[ts:ts + MAX_WINDOW, :]

    hext_ref[MAX_WINDOW:MAX_WINDOW + ts, :] = h
    pos = s * ts + lax.broadcasted_iota(I32, (ts, dg), 0)
    for gi, w in enumerate(POOL_WINDOWS):
        c0 = gi * dg
        acc = h[:, c0:c0 + dg]
        for k in range(1, w):
            acc = acc + hext_ref[MAX_WINDOW - k:MAX_WINDOW - k + ts, c0:c0 + dg]
        count = jnp.minimum(pos + 1, w).astype(F32)
        pooled = acc / count - h[:, c0:c0 + dg]
        mixed = jnp.dot(pooled.astype(BF16), pw_ref[gi], preferred_element_type=F32)
        o_ref[0, :, c0:c0 + dg] = (x[:, c0:c0 + dg]
                                   + gt_ref[0][:, c0:c0 + dg] * (mixed * ps_ref[:, c0:c0 + dg]))


def _pool_call(x, shift, scale, gate, norm_g, pool_w, pool_scale):
    b, s, d = x.shape
    ng, dg, _ = pool_w.shape
    ts = min(s, 256)
    vec = pl.BlockSpec((1, 1, d), lambda bi, si: (bi, 0, 0))
    row = pl.BlockSpec((1, d), lambda bi, si: (0, 0))
    return pl.pallas_call(
        functools.partial(_pool_kernel, ts=ts, dg=dg),
        grid=(b, s // ts),
        in_specs=[pl.BlockSpec((1, ts, d), lambda bi, si: (bi, si, 0)), vec, vec, vec, row,
                  pl.BlockSpec((ng, dg, dg), lambda bi, si: (0, 0, 0)), row],
        out_specs=pl.BlockSpec((1, ts, d), lambda bi, si: (bi, si, 0)),
        out_shape=jax.ShapeDtypeStruct((b, s, d), F32),
        scratch_shapes=[pltpu.VMEM((ts + MAX_WINDOW, d), F32)],
        compiler_params=_params("arbitrary", "arbitrary"),
        name="pool_mixer",
    )(x, shift, scale, gate, norm_g.reshape(1, d), pool_w.astype(BF16), pool_scale.reshape(1, d))


def _qkv_kernel(x_ref, sh_ref, sc_ref, g_ref, w_ref, ng_ref, o_ref, h_ref, res_ref, *, n_norm_tiles, heads):
    j = pl.program_id(2)
    tm = h_ref.shape[0]
    rows = min(tm, 256)
    sub = min(rows, 64)

    @pl.when(j == 0)
    def _():
        h_ref[...] = _modulate(x_ref[0], g_ref[...], sh_ref[0], sc_ref[0]).astype(BF16)

    is_norm = j < n_norm_tiles
    g = ng_ref[0]
    for r0 in range(0, tm, rows):
        res_ref[r0:r0 + rows, :] = jnp.dot(h_ref[r0:r0 + rows, :], w_ref[...], preferred_element_type=F32)
        for hh in range(heads):
            for r1 in range(r0, r0 + rows, sub):
                blk = res_ref[r1:r1 + sub, hh * HEAD_DIM:(hh + 1) * HEAD_DIM]
                inv = lax.rsqrt(jnp.mean(blk * blk, axis=-1, keepdims=True) + NORM_EPS)
                y = jnp.where(is_norm, (blk * inv) * g, blk)
                o_ref[hh, 0, r1:r1 + sub, :] = y.astype(BF16)


def _qkv_call(x, shift, scale, norm_g, w_qkv, q_norm_g, k_norm_g):
    b, s, d = x.shape
    n = w_qkv.shape[1]
    tm = min(s, 512)
    tn = min(d, 1024)
    heads = tn // HEAD_DIM
    tiles_per_part = d // tn
    qk_gain = jnp.stack([q_norm_g * (HEAD_DIM ** -0.5), k_norm_g, jnp.ones_like(k_norm_g)])
    vec = pl.BlockSpec((1, 1, d), lambda bi, si, j: (bi, 0, 0))
    return pl.pallas_call(
        functools.partial(_qkv_kernel, n_norm_tiles=2 * tiles_per_part, heads=heads),
        grid=(b, s // tm, n // tn),
        in_specs=[pl.BlockSpec((1, tm, d), lambda bi, si, j: (bi, si, 0)), vec, vec,
                  pl.BlockSpec((1, d), lambda bi, si, j: (0, 0)),
                  pl.BlockSpec((d, tn), lambda bi, si, j: (0, j)),
                  pl.BlockSpec((1, 1, HEAD_DIM), lambda bi, si, j: (j // tiles_per_part, 0, 0))],
        out_specs=pl.BlockSpec((heads, 1, tm, HEAD_DIM), lambda bi, si, j: (j, bi, si, 0)),
        out_shape=jax.ShapeDtypeStruct((n // HEAD_DIM, b, s, HEAD_DIM), BF16),
        scratch_shapes=[pltpu.VMEM((tm, d), BF16), pltpu.VMEM((tm, tn), F32)],
        compiler_params=_params("arbitrary", "arbitrary", "arbitrary"),
        name="qkv_proj",
    )(x, shift, scale, norm_g.reshape(1, d), w_qkv.astype(BF16), qk_gain.reshape(3, 1, HEAD_DIM))


def _attn_kernel(q_ref, k_ref, v_ref, o_ref, *, tq, heads):
    qi = pl.program_id(2)
    row = lax.broadcasted_iota(I32, (tq, tq), 0)
    col = lax.broadcasted_iota(I32, (tq, tq), 1)
    suffix_sum = (row >= col).astype(BF16)
    causal = col < row

    def block(j, accs, rs, masked):
        start = pl.multiple_of(j * tq, tq)
        new_accs, new_rs = [], []
        for g in range(heads):
            k = k_ref[g, 0, pl.ds(start, tq), :]
            v = v_ref[g, 0, pl.ds(start, tq), :]
            z = lax.dot_general(q_ref[g, 0], k, (((1,), (1,)), ((), ())), preferred_element_type=F32)
            log_fail = -(jnp.maximum(z, 0.0) + jnp.log(1.0 + jnp.exp(-jnp.abs(z))))
            if masked:
                log_fail = jnp.where(causal, log_fail, 0.0)
            cum = jnp.dot(log_fail.astype(BF16), suffix_sum, preferred_element_type=F32)
            a = jnp.exp(z + cum + rs[g])
            if masked:
                a = jnp.where(causal, a, 0.0)
            new_accs.append(accs[g] + jnp.dot(a.astype(BF16), v, preferred_element_type=F32))
            new_rs.append(rs[g] + cum[:, 0:1])
        return tuple(new_accs), tuple(new_rs)

    def slowest(rs):
        m = jnp.max(rs[0])
        for r in rs[1:]:
            m = jnp.maximum(m, jnp.max(r))
        return m

    accs, rs = block(qi, (jnp.zeros((tq, HEAD_DIM), F32),) * heads, (jnp.zeros((tq, 1), F32),) * heads, True)

    def cond(carry):
        j, _, _, r_max = carry
        return jnp.logical_and(j >= 0, r_max > LOG_F32_ZERO)

    def body(carry):
        j, accs, rs, _ = carry
        accs, rs = block(j, accs, rs, False)
        return j - 1, accs, rs, slowest(rs)

    _, accs, _, _ = lax.while_loop(cond, body, (qi - 1, accs, rs, slowest(rs)))
    for g in range(heads):
        o_ref[0, :, g * HEAD_DIM:(g + 1) * HEAD_DIM] = accs[g].astype(o_ref.dtype)


def _attn_call(qkv, d):
    n3h, b, s, _ = qkv.shape
    nh = n3h // 3
    tq = min(s, 256)
    heads = 4 if nh % 4 == 0 else 1
    ng = nh // heads
    return pl.pallas_call(
        functools.partial(_attn_kernel, tq=tq, heads=heads),
        grid=(b, ng, s // tq),
        in_specs=[pl.BlockSpec((heads, 1, tq, HEAD_DIM), lambda bi, h, qi: (h, bi, qi, 0)),
                  pl.BlockSpec((heads, 1, s, HEAD_DIM), lambda bi, h, qi: (ng + h, bi, 0, 0)),
                  pl.BlockSpec((heads, 1, s, HEAD_DIM), lambda bi, h, qi: (2 * ng + h, bi, 0, 0))],
        out_specs=pl.BlockSpec((1, tq, heads * HEAD_DIM), lambda bi, h, qi: (bi, qi, h)),
        out_shape=jax.ShapeDtypeStruct((b, s, d), BF16),
        compiler_params=_params("arbitrary", "arbitrary", "arbitrary"),
        name="stickbreak_attn",
    )(qkv, qkv, qkv)


def _oproj_kernel(o_ref, w_ref, x_ref, gt_ref, out_ref):
    tm, d = o_ref.shape[1], w_ref.shape[1]
    rows, cols = min(tm, 256), min(d, 512)
    for r0 in range(0, tm, rows):
        for c0 in range(0, d, cols):
            res = jnp.dot(o_ref[0, r0:r0 + rows, :], w_ref[:, c0:c0 + cols], preferred_element_type=F32)
            out_ref[0, r0:r0 + rows, c0:c0 + cols] = (x_ref[0, r0:r0 + rows, c0:c0 + cols]
                                                      + gt_ref[0][:, c0:c0 + cols] * res)


def _oproj_call(o, w_o, x, gate):
    b, s, d = x.shape
    tm = min(s, 512)
    return pl.pallas_call(
        _oproj_kernel,
        grid=(b, s // tm),
        in_specs=[pl.BlockSpec((1, tm, d), lambda bi, si: (bi, si, 0)),
                  pl.BlockSpec((d, d), lambda bi, si: (0, 0)),
                  pl.BlockSpec((1, tm, d), lambda bi, si: (bi, si, 0)),
                  pl.BlockSpec((1, 1, d), lambda bi, si: (bi, 0, 0))],
        out_specs=pl.BlockSpec((1, tm, d), lambda bi, si: (bi, si, 0)),
        out_shape=jax.ShapeDtypeStruct((b, s, d), F32),
        compiler_params=_params("arbitrary", "arbitrary"),
        name="attn_out_proj",
    )(o, w_o.astype(BF16), x, gate)


def _split_bf16(a):
    hi = a.astype(BF16)
    return hi, (a - hi.astype(F32)).astype(BF16)


def _router_kernel(x_ref, sh_ref, sc_ref, g_ref, wr_ref, br_ref, h_ref, route_ref, idx_ref, cnt_ref,
                   *, ts, n_groups, n_exp):
    first = jnp.logical_and(pl.program_id(0) == 0, pl.program_id(1) == 0)

    @pl.when(first)
    def _():
        cnt_ref[...] = jnp.zeros_like(cnt_ref)

    h = _modulate(x_ref[0], g_ref[...], sh_ref[0], sc_ref[0])
    h_ref[0] = h

    h_hi, h_lo = _split_bf16(h)
    w_hi, w_lo = _split_bf16(wr_ref[...])
    logits = (jnp.dot(h_hi, w_hi, preferred_element_type=F32)
              + jnp.dot(h_hi, w_lo, preferred_element_type=F32)
              + jnp.dot(h_lo, w_hi, preferred_element_type=F32)) + br_ref[...]

    lane = lax.broadcasted_iota(I32, (ts, LANES), 1)
    neg = jnp.float32(-jnp.inf)

    def first_argmax(vals):
        m = jnp.max(vals, axis=-1, keepdims=True)
        idx = jnp.min(jnp.where(vals == m, lane, LANES), axis=-1, keepdims=True)
        return m, idx

    is_grp = lane < n_groups
    g_max, g_idx = first_argmax(jnp.where(is_grp, logits, neg))
    g_sum = jnp.sum(jnp.where(is_grp, jnp.exp(logits - g_max), 0.0), axis=-1, keepdims=True)
    p_grp = 1.0 / g_sum
    lo = n_groups + g_idx * n_exp
    in_grp = jnp.logical_and(lane >= lo, lane < lo + n_exp)
    sel = jnp.where(in_grp, logits, neg)
    v1, i1 = first_argmax(sel)
    v2, i2 = first_argmax(jnp.where(lane == i1, neg, sel))
    e21 = jnp.exp(v2 - v1)
    w1 = p_grp * (1.0 / (1.0 + e21))
    w2 = p_grp * (e21 / (1.0 + e21))

    oh1 = (lane == i1).astype(F32)
    oh2 = (lane == i2).astype(F32)
    both = oh1 + oh2
    earlier = (lax.broadcasted_iota(I32, (ts, ts), 1) < lax.broadcasted_iota(I32, (ts, ts), 0))
    ranks = jnp.dot(earlier.astype(BF16), both.astype(BF16), preferred_element_type=F32) + cnt_ref[...]
    r1 = jnp.sum(oh1 * ranks, axis=-1, keepdims=True)
    r2 = jnp.sum(oh2 * ranks, axis=-1, keepdims=True)
    cnt_ref[...] = cnt_ref[...] + jnp.sum(both, axis=0, keepdims=True)

    out = jnp.zeros((ts, LANES), F32)
    for k, val in enumerate(((i1 - n_groups).astype(F32), (i2 - n_groups).astype(F32), w1, w2, r1, r2)):
        out = jnp.where(lane == k, val, out)
    route_ref[0] = out
    idx_ref[0] = out.T[0:ROUTE_FIELDS, :].astype(I32)


def _router_call(x, shift, scale, norm_g, w_grp, b_grp, w_exp, b_exp):
    b, s, d = x.shape
    n_groups, n_exp = b_exp.shape
    n_logits = n_groups + n_groups * n_exp
    ts = min(s, 256)
    tiles = s // ts
    w_r = jnp.concatenate([w_grp, w_exp.reshape(d, n_groups * n_exp),
                           jnp.zeros((d, LANES - n_logits), F32)], axis=1)
    b_r = jnp.concatenate([b_grp, b_exp.reshape(-1), jnp.zeros((LANES - n_logits,), F32)]).reshape(1, LANES)
    vec = pl.BlockSpec((1, 1, d), lambda bi, si: (bi, 0, 0))
    return pl.pallas_call(
        functools.partial(_router_kernel, ts=ts, n_groups=n_groups, n_exp=n_exp),
        grid=(b, s // ts),
        in_specs=[pl.BlockSpec((1, ts, d), lambda bi, si: (bi, si, 0)), vec, vec,
                  pl.BlockSpec((1, d), lambda bi, si: (0, 0)),
                  pl.BlockSpec((d, LANES), lambda bi, si: (0, 0)),
                  pl.BlockSpec((1, LANES), lambda bi, si: (0, 0))],
        out_specs=[pl.BlockSpec((1, ts, d), lambda bi, si: (bi, si, 0)),
                   pl.BlockSpec((1, ts, LANES), lambda bi, si: (bi, si, 0)),
                   pl.BlockSpec((1, ROUTE_FIELDS, ts), lambda bi, si: (bi * tiles + si, 0, 0)),
                   pl.BlockSpec((1, LANES), lambda bi, si: (0, 0))],
        out_shape=[jax.ShapeDtypeStruct((b, s, d), F32),
                   jax.ShapeDtypeStruct((b, s, LANES), F32),
                   jax.ShapeDtypeStruct((b * tiles, ROUTE_FIELDS, ts), I32),
                   jax.ShapeDtypeStruct((1, LANES), F32)],
        compiler_params=_params("arbitrary", "arbitrary"),
        name="moe_router",
    )(x, shift, scale, norm_g.reshape(1, d), w_r, b_r)


def _row_gather_start(src_hbm, dst, sem, idx_ref, base):
    n_rows = dst.shape[0]

    def body(c, carry):
        for u in range(DMA_ISSUE_UNROLL):
            r = c * DMA_ISSUE_UNROLL + u
            pltpu.make_async_copy(src_hbm.at[pl.ds(idx_ref[base + r], 1)], dst.at[pl.ds(r, 1)], sem).start()
        return carry
    lax.fori_loop(0, n_rows // DMA_ISSUE_UNROLL, body, 0)


def _row_gather_inline(src_hbm, dst, sem, idx_ref, base):
    for r in range(dst.shape[0]):
        pltpu.make_async_copy(src_hbm.at[pl.ds(idx_ref[base + r], 1)], dst.at[pl.ds(r, 1)], sem).start()


def _row_gather_wait(src_hbm, dst, sem):
    pltpu.make_async_copy(src_hbm.at[pl.ds(0, dst.shape[0])], dst, sem).wait()


def _expert_kernel(pos_ref, off_ref, cnt_ref, nact_ref, h_hbm, wg_ref, wu_ref, wd_ref, y_hbm,
                   xbuf, ybuf, gsem, ysem, tok_ref, wg_bf, wu_bf, wd_bf, *, tm, n_tok, n_experts, n_tiles):
    expert = pl.program_id(0)
    n_active = nact_ref[0]
    tiles_e = (cnt_ref[expert] + tm - 1) // tm
    first_tile = off_ref[expert] // tm

    def y_write(slot, g):
        return pltpu.make_async_copy(ybuf.at[slot], y_hbm.at[pl.ds(g * tm, tm)], ysem.at[slot])

    @pl.when(expert == 0)
    def _():
        def pad_expert(e, carry):
            first = off_ref[e] + cnt_ref[e]
            last = off_ref[e] + ((cnt_ref[e] + tm - 1) // tm) * tm

            def pad(p, c):
                tok_ref[p] = 0
                return c
            return lax.fori_loop(first, last, pad, carry)
        lax.fori_loop(0, n_experts, pad_expert, 0)

        def place(c, carry):
            toks = [c * DMA_ISSUE_UNROLL + u for u in range(DMA_ISSUE_UNROLL)]
            rows = [(pos_ref[i], pos_ref[n_tok + i]) for i in toks]
            for i, (a, b) in zip(toks, rows):
                tok_ref[a] = i
                tok_ref[b] = i
            return carry
        lax.fori_loop(0, n_tok // DMA_ISSUE_UNROLL, place, 0)
        for k in range(GATHER_SLOTS - 1):
            _row_gather_start(h_hbm, xbuf.at[k], gsem.at[k], tok_ref, jnp.minimum(k, n_active - 1) * tm)

    @pl.when(tiles_e > 0)
    def _():
        wg_bf[...] = wg_ref[0].astype(BF16)
        wu_bf[...] = wu_ref[0].astype(BF16)
        wd_bf[...] = wd_ref[0].astype(BF16)

    def tile_body(i, carry):
        g = first_tile + i
        xs = g % GATHER_SLOTS
        ys = g % 2

        _row_gather_wait(h_hbm, xbuf.at[xs], gsem.at[xs])
        ahead = (g + GATHER_SLOTS - 1) % GATHER_SLOTS
        _row_gather_inline(h_hbm, xbuf.at[ahead], gsem.at[ahead], tok_ref,
                           jnp.minimum(g + GATHER_SLOTS - 1, n_active - 1) * tm)
        x = xbuf[xs].astype(BF16)
        a = jnp.dot(x, wg_bf[...], preferred_element_type=F32)
        u = jnp.dot(x, wu_bf[...], preferred_element_type=F32)
        hid = (a * (1.0 / (1.0 + jnp.exp(-a)))) * u

        @pl.when(g >= 2)
        def _():
            y_write(ys, g).wait()

        ybuf[ys] = jnp.dot(hid.astype(BF16), wd_bf[...], preferred_element_type=F32)
        y_write(ys, g).start()
        return carry

    lax.fori_loop(0, tiles_e, tile_body, 0)

    @pl.when(expert == n_experts - 1)
    def _():
        @pl.when(n_active >= 2)
        def _():
            y_write(n_active % 2, 0).wait()

        y_write((n_active + 1) % 2, 0).wait()
        for k in range(GATHER_SLOTS - 1):
            dangling = (n_active + k) % GATHER_SLOTS
            _row_gather_wait(h_hbm, xbuf.at[dangling], gsem.at[dangling])

        xbuf[0] = jnp.zeros(xbuf.shape[1:], F32)

        def zero_tile(g, carry):
            pltpu.make_async_copy(xbuf.at[0], y_hbm.at[pl.ds(g * tm, tm)], gsem.at[0]).start()
            return carry

        def zero_wait(g, carry):
            pltpu.make_async_copy(xbuf.at[0], y_hbm.at[pl.ds(0, tm)], gsem.at[0]).wait()
            return carry
        lax.fori_loop(n_active, n_tiles, zero_tile, 0)
        lax.fori_loop(n_active, n_tiles, zero_wait, 0)


def _expert_call(h2, pos, offset, cnt, n_active, n_tiles, w_gate, w_up, w_down, layer, tm):
    t, d = h2.shape
    n_all, _, f = w_gate.shape
    experts_per_layer = n_all // layer[1]
    base = layer[0] * experts_per_layer

    def w_map(e, *_):
        return (base + e, 0, 0)

    grid_spec = pltpu.PrefetchScalarGridSpec(
        num_scalar_prefetch=4,
        grid=(experts_per_layer,),
        in_specs=[pl.BlockSpec(memory_space=pl.ANY),
                  pl.BlockSpec((1, d, f), w_map),
                  pl.BlockSpec((1, d, f), w_map),
                  pl.BlockSpec((1, f, d), w_map)],
        out_specs=pl.BlockSpec(memory_space=pl.ANY),
        scratch_shapes=[pltpu.VMEM((GATHER_SLOTS, tm, d), F32),
                        pltpu.VMEM((2, tm, d), F32),
                        pltpu.SemaphoreType.DMA((GATHER_SLOTS,)),
                        pltpu.SemaphoreType.DMA((2,)),
                        pltpu.SMEM((n_tiles * tm,), I32),
                        pltpu.VMEM((d, f), BF16), pltpu.VMEM((d, f), BF16), pltpu.VMEM((f, d), BF16)],
    )
    return pl.pallas_call(
        functools.partial(_expert_kernel, tm=tm, n_tok=t, n_experts=experts_per_layer, n_tiles=n_tiles),
        grid_spec=grid_spec,
        out_shape=jax.ShapeDtypeStruct((n_tiles * tm, d), F32),
        compiler_params=_params("arbitrary"),
        name="moe_experts",
    )(pos, offset, cnt, n_active, h2, w_gate, w_up, w_down)


def _combine_kernel(pos_ref, x_ref, route_ref, gt_ref, y_hbm, o_ref, ybuf, sem, *, ts, tiles_per_seq):
    b = pl.program_id(0)
    s = pl.program_id(1)
    step = b * tiles_per_seq + s
    n_steps = pl.num_programs(0) * tiles_per_seq
    n_tok = n_steps * ts

    @pl.when(step == 0)
    def _():
        _row_gather_start(y_hbm, ybuf.at[0, 0], sem.at[0], pos_ref, 0)
        _row_gather_start(y_hbm, ybuf.at[0, 1], sem.at[0], pos_ref, n_tok)

    slot = step % 2
    nxt = jnp.minimum(step + 1, n_steps - 1) * ts
    _row_gather_inline(y_hbm, ybuf.at[1 - slot, 0], sem.at[1 - slot], pos_ref, nxt)
    _row_gather_inline(y_hbm, ybuf.at[1 - slot, 1], sem.at[1 - slot], pos_ref, n_tok + nxt)
    _row_gather_wait(y_hbm, ybuf.at[slot, 0], sem.at[slot])
    _row_gather_wait(y_hbm, ybuf.at[slot, 1], sem.at[slot])
    route = route_ref[0]
    moe = route[:, 2:3] * ybuf[slot, 0] + route[:, 3:4] * ybuf[slot, 1]
    o_ref[0] = x_ref[0] + gt_ref[0] * moe

    @pl.when(step == n_steps - 1)
    def _():
        _row_gather_wait(y_hbm, ybuf.at[1 - slot, 0], sem.at[1 - slot])
        _row_gather_wait(y_hbm, ybuf.at[1 - slot, 1], sem.at[1 - slot])


def _combine_call(x, route, gate, y, pos):
    b, s, d = x.shape
    ts = min(s, 256)
    tiles_per_seq = s // ts
    grid_spec = pltpu.PrefetchScalarGridSpec(
        num_scalar_prefetch=1,
        grid=(b, tiles_per_seq),
        in_specs=[pl.BlockSpec((1, ts, d), lambda bi, si, pos: (bi, si, 0)),
                  pl.BlockSpec((1, ts, LANES), lambda bi, si, pos: (bi, si, 0)),
                  pl.BlockSpec((1, 1, d), lambda bi, si, pos: (bi, 0, 0)),
                  pl.BlockSpec(memory_space=pl.ANY)],
        out_specs=pl.BlockSpec((1, ts, d), lambda bi, si, pos: (bi, si, 0)),
        scratch_shapes=[pltpu.VMEM((2, 2, ts, d), F32), pltpu.SemaphoreType.DMA((2,))],
    )
    return pl.pallas_call(
        functools.partial(_combine_kernel, ts=ts, tiles_per_seq=tiles_per_seq),
        grid_spec=grid_spec,
        out_shape=jax.ShapeDtypeStruct((b, s, d), F32),
        compiler_params=_params("arbitrary", "arbitrary"),
        name="moe_combine",
    )(pos, x, route, gate, y)


def _moe(x, shift, scale, gate, norm_g, w_grp, b_grp, w_exp, b_exp, w_gate, w_up, w_down, layer):
    b, s, d = x.shape
    t = b * s
    n_groups, n_exp = b_exp.shape
    n_experts = n_groups * n_exp
    tm = min(t, 256)

    h2, route, idx, counts = _router_call(x, shift, scale, norm_g, w_grp, b_grp, w_exp, b_exp)

    cnt = counts[0, n_groups:n_groups + n_experts].astype(I32)
    tiles_e = (cnt + tm - 1) // tm
    tiles_end = jnp.cumsum(tiles_e)
    offset = (tiles_end - tiles_e) * tm
    n_tiles = (2 * t + n_experts * (tm - 1) + tm - 1) // tm
    n_active = tiles_end[-1:]
    fields = idx.transpose(1, 0, 2).reshape(ROUTE_FIELDS, t)
    expert = jnp.concatenate([fields[0], fields[1]])
    rank = jnp.concatenate([fields[4], fields[5]])
    is_e = expert[None, :] == jnp.arange(n_experts, dtype=I32)[:, None]
    pos = jnp.sum(jnp.where(is_e, offset[:, None], 0), axis=0) + rank

    y = _expert_call(h2.reshape(t, d), pos, offset, cnt, n_active, n_tiles,
                     w_gate, w_up, w_down, layer, tm)
    return _combine_call(x, route, gate, y, pos)


def kernel(x, c, ada_w, ada_b, norm1_g, norm2_g, pool_w, pool_scale, w_qkv, w_o, q_norm_g, k_norm_g,
           w_grp, b_grp, w_exp, b_exp, w_gate, w_up, w_down):
    depth = ada_w.shape[0]
    b, s, d = x.shape
    n_mixers = 2
    mod = _ada_call(c, ada_w, ada_b).reshape(depth, b, 6, 1, d)
    wshape = w_gate.shape
    w_gate_s = w_gate.reshape((-1,) + wshape[3:])
    w_up_s = w_up.reshape((-1,) + wshape[3:])
    w_down_s = w_down.reshape((-1,) + w_down.shape[3:])
    for i in range(depth):
        shift1, scale1, gate1, shift2, scale2, gate2 = (mod[i, :, k] for k in range(6))
        j = i // n_mixers
        if i % n_mixers == 0:
            x = _pool_call(x, shift1, scale1, gate1, norm1_g[i], pool_w[j], pool_scale[j])
        else:
            qkv = _qkv_call(x, shift1, scale1, norm1_g[i], w_qkv[j], q_norm_g[j], k_norm_g[j])
            o = _attn_call(qkv, d)
            x = _oproj_call(o, w_o[j], x, gate1)
        x = _moe(x, shift2, scale2, gate2, norm2_g[i], w_grp[i], b_grp[i], w_exp[i], b_exp[i],
                 w_gate_s, w_up_s, w_down_s, (i, depth))
    return x
```

```python
import functools

import jax
import jax.numpy as jnp
from jax import lax
from jax.experimental import pallas as pl
from jax.experimental.pallas import tpu as pltpu

F32 = jnp.float32
BF16 = jnp.bfloat16
I32 = jnp.int32

NORM_EPS = 1e-6
POOL_WINDOWS = (2, 4, 8, 16)
MAX_WINDOW = max(POOL_WINDOWS)
HEAD_DIM = 128
LANES = 128
ROUTE_FIELDS = 8
DMA_ISSUE_UNROLL = 8
GATHER_SLOTS = 3
GATHER_CHUNK = 64
VMEM_LIMIT_BYTES = 56 * 1024 * 1024
LOG_F32_ZERO = -110.0


def _params(*sem):
    return pltpu.CompilerParams(dimension_semantics=sem, vmem_limit_bytes=VMEM_LIMIT_BYTES)


def _modulate(x, g, shift, scale):
    y = x * lax.rsqrt(jnp.mean(x * x, axis=-1, keepdims=True) + NORM_EPS)
    return (y * g) * (1.0 + scale) + shift


def _ada_kernel(c_ref, w_ref, b_ref, o_ref):
    c = c_ref[...]
    c_act = c * (1.0 / (1.0 + jnp.exp(-c)))
    o_ref[0] = jnp.dot(c_act.astype(BF16), w_ref[0].astype(BF16),
                       preferred_element_type=F32) + b_ref[0]


def _ada_call(c, ada_w, ada_b):
    depth, d, n = ada_w.shape
    b = c.shape[0]
    tn = min(n, 1024)
    return pl.pallas_call(
        _ada_kernel,
        grid=(depth, n // tn),
        in_specs=[pl.BlockSpec((b, d), lambda l, j: (0, 0)),
                  pl.BlockSpec((1, d, tn), lambda l, j: (l, 0, j)),
                  pl.BlockSpec((1, 1, tn), lambda l, j: (l, 0, j))],
        out_specs=pl.BlockSpec((1, b, tn), lambda l, j: (l, 0, j)),
        out_shape=jax.ShapeDtypeStruct((depth, b, n), F32),
        compiler_params=_params("arbitrary", "arbitrary"),
        name="ada",
    )(c, ada_w, ada_b.reshape(depth, 1, n))


def _pool_kernel(x_ref, sh_ref, sc_ref, gt_ref, g_ref, pw_ref, ps_ref, o_ref, hext_ref, *, ts, dg):
    s = pl.program_id(1)
    x = x_ref[0]
    h = _modulate(x, g_ref[...], sh_ref[0], sc_ref[0])

    @pl.when(s == 0)
    def _():
        hext_ref[0:MAX_WINDOW, :] = jnp.zeros((MAX_WINDOW, h.shape[1]), F32)

    @pl.when(s > 0)
    def _():
        hext_ref[0:MAX_WINDOW, :] = hext_ref[ts:ts + MAX_WINDOW, :]

    hext_ref[MAX_WINDOW:MAX_WINDOW + ts, :] = h
    pos = s * ts + lax.broadcasted_iota(I32, (ts, dg), 0)
    for gi, w in enumerate(POOL_WINDOWS):
        c0 = gi * dg
        acc = h[:, c0:c0 + dg]
        for k in range(1, w):
            acc = acc + hext_ref[MAX_WINDOW - k:MAX_WINDOW - k + ts, c0:c0 + dg]
        count = jnp.minimum(pos + 1, w).astype(F32)
        pooled = acc / count - h[:, c0:c0 + dg]
        mixed = jnp.dot(pooled.astype(BF16), pw_ref[gi], preferred_element_type=F32)
        o_ref[0, :, c0:c0 + dg] = (x[:, c0:c0 + dg]
                                   + gt_ref[0][:, c0:c0 + dg] * (mixed * ps_ref[:, c0:c0 + dg]))


def _pool_call(x, shift, scale, gate, norm_g, pool_w, pool_scale):
    b, s, d = x.shape
    ng, dg, _ = pool_w.shape
    ts = min(s, 256)
    vec = pl.BlockSpec((1, 1, d), lambda bi, si: (bi, 0, 0))
    row = pl.BlockSpec((1, d), lambda bi, si: (0, 0))
    return pl.pallas_call(
        functools.partial(_pool_kernel, ts=ts, dg=dg),
        grid=(b, s // ts),
        in_specs=[pl.BlockSpec((1, ts, d), lambda bi, si: (bi, si, 0)), vec, vec, vec, row,
                  pl.BlockSpec((ng, dg, dg), lambda bi, si: (0, 0, 0)), row],
        out_specs=pl.BlockSpec((1, ts, d), lambda bi, si: (bi, si, 0)),
        out_shape=jax.ShapeDtypeStruct((b, s, d), F32),
        scratch_shapes=[pltpu.VMEM((ts + MAX_WINDOW, d), F32)],
        compiler_params=_params("arbitrary", "arbitrary"),
        name="pool_mixer",
    )(x, shift, scale, gate, norm_g.reshape(1, d), pool_w.astype(BF16), pool_scale.reshape(1, d))


def _qkv_kernel(x_ref, sh_ref, sc_ref, g_ref, w_ref, ng_ref, o_ref, h_ref, res_ref, *, n_norm_tiles, heads):
    j = pl.program_id(2)
    tm = h_ref.shape[0]
    rows = min(tm, 256)
    sub = min(rows, 64)

    @pl.when(j == 0)
    def _():
        h_ref[...] = _modulate(x_ref[0], g_ref[...], sh_ref[0], sc_ref[0]).astype(BF16)

    is_norm = j < n_norm_tiles
    g = ng_ref[0]
    for r0 in range(0, tm, rows):
        res_ref[r0:r0 + rows, :] = jnp.dot(h_ref[r0:r0 + rows, :], w_ref[...], preferred_element_type=F32)
        for hh in range(heads):
            for r1 in range(r0, r0 + rows, sub):
                blk = res_ref[r1:r1 + sub, hh * HEAD_DIM:(hh + 1) * HEAD_DIM]
                inv = lax.rsqrt(jnp.mean(blk * blk, axis=-1, keepdims=True) + NORM_EPS)
                y = jnp.where(is_norm, (blk * inv) * g, blk)
                o_ref[hh, 0, r1:r1 + sub, :] = y.astype(BF16)


def _qkv_call(x, shift, scale, norm_g, w_qkv, q_norm_g, k_norm_g):
    b, s, d = x.shape
    n = w_qkv.shape[1]
    tm = min(s, 512)
    tn = min(d, 1024)
    heads = tn // HEAD_DIM
    tiles_per_part = d // tn
    qk_gain = jnp.stack([q_norm_g * (HEAD_DIM ** -0.5), k_norm_g, jnp.ones_like(k_norm_g)])
    vec = pl.BlockSpec((1, 1, d), lambda bi, si, j: (bi, 0, 0))
    return pl.pallas_call(
        functools.partial(_qkv_kernel, n_norm_tiles=2 * tiles_per_part, heads=heads),
        grid=(b, s // tm, n // tn),
        in_specs=[pl.BlockSpec((1, tm, d), lambda bi, si, j: (bi, si, 0)), vec, vec,
                  pl.BlockSpec((1, d), lambda bi, si, j: (0, 0)),
                  pl.BlockSpec((d, tn), lambda bi, si, j: (0, j)),
                  pl.BlockSpec((1, 1, HEAD_DIM), lambda bi, si, j: (j // tiles_per_part, 0, 0))],
        out_specs=pl.BlockSpec((heads, 1, tm, HEAD_DIM), lambda bi, si, j: (j, bi, si, 0)),
        out_shape=jax.ShapeDtypeStruct((n // HEAD_DIM, b, s, HEAD_DIM), BF16),
        scratch_shapes=[pltpu.VMEM((tm, d), BF16), pltpu.VMEM((tm, tn), F32)],
        compiler_params=_params("arbitrary", "arbitrary", "arbitrary"),
        name="qkv_proj",
    )(x, shift, scale, norm_g.reshape(1, d), w_qkv.astype(BF16), qk_gain.reshape(3, 1, HEAD_DIM))


def _attn_kernel(q_ref, k_ref, v_ref, o_ref, *, tq, heads):
    qi = pl.program_id(2)
    row = lax.broadcasted_iota(I32, (tq, tq), 0)
    col = lax.broadcasted_iota(I32, (tq, tq), 1)
    suffix_sum = (row >= col).astype(BF16)
    causal = col < row

    def block(j, accs, rs, masked):
        start = pl.multiple_of(j * tq, tq)
        new_accs, new_rs = [], []
        for g in range(heads):
            k = k_ref[g, 0, pl.ds(start, tq), :]
            v = v_ref[g, 0, pl.ds(start, tq), :]
            z = lax.dot_general(q_ref[g, 0], k, (((1,), (1,)), ((), ())), preferred_element_type=F32)
            log_fail = -(jnp.maximum(z, 0.0) + jnp.log(1.0 + jnp.exp(-jnp.abs(z))))
            if masked:
                log_fail = jnp.where(causal, log_fail, 0.0)
            cum = jnp.dot(log_fail.astype(BF16), suffix_sum, preferred_element_type=F32)
            a = jnp.exp(z + cum + rs[g])
            if masked:
                a = jnp.where(causal, a, 0.0)
            new_accs.append(accs[g] + jnp.dot(a.astype(BF16), v, preferred_element_type=F32))
            new_rs.append(rs[g] + cum[:, 0:1])
        return tuple(new_accs), tuple(new_rs)

    def slowest(rs):
        m = jnp.max(rs[0])
        for r in rs[1:]:
            m = jnp.maximum(m, jnp.max(r))
        return m

    accs, rs = block(qi, (jnp.zeros((tq, HEAD_DIM), F32),) * heads, (jnp.zeros((tq, 1), F32),) * heads, True)

    def cond(carry):
        j, _, _, r_max = carry
        return jnp.logical_and(j >= 0, r_max > LOG_F32_ZERO)

    def body(carry):
        j, accs, rs, _ = carry
        accs, rs = block(j, accs, rs, False)
        return j - 1, accs, rs, slowest(rs)

    _, accs, _, _ = lax.while_loop(cond, body, (qi - 1, accs, rs, slowest(rs)))
    for g in range(heads):
        o_ref[0, :, g * HEAD_DIM:(g + 1) * HEAD_DIM] = accs[g].astype(o_ref.dtype)


def _attn_call(qkv, d):
    n3h, b, s, _ = qkv.shape
    nh = n3h // 3
    tq = min(s, 256)
    heads = 4 if nh % 4 == 0 else 1
    ng = nh // heads
    return pl.pallas_call(
        functools.partial(_attn_kernel, tq=tq, heads=heads),
        grid=(b, ng, s // tq),
        in_specs=[pl.BlockSpec((heads, 1, tq, HEAD_DIM), lambda bi, h, qi: (h, bi, qi, 0)),
                  pl.BlockSpec((heads, 1, s, HEAD_DIM), lambda bi, h, qi: (ng + h, bi, 0, 0)),
                  pl.BlockSpec((heads, 1, s, HEAD_DIM), lambda bi, h, qi: (2 * ng + h, bi, 0, 0))],
        out_specs=pl.BlockSpec((1, tq, heads * HEAD_DIM), lambda bi, h, qi: (bi, qi, h)),
        out_shape=jax.ShapeDtypeStruct((b, s, d), BF16),
        compiler_params=_params("arbitrary", "arbitrary", "arbitrary"),
        name="stickbreak_attn",
    )(qkv, qkv, qkv)


def _oproj_kernel(o_ref, w_ref, x_ref, gt_ref, out_ref):
    tm, d = o_ref.shape[1], w_ref.shape[1]
    rows, cols = min(tm, 256), min(d, 512)
    for r0 in range(0, tm, rows):
        for c0 in range(0, d, cols):
            res = jnp.dot(o_ref[0, r0:r0 + rows, :], w_ref[:, c0:c0 + cols], preferred_element_type=F32)
            out_ref[0, r0:r0 + rows, c0:c0 + cols] = (x_ref[0, r0:r0 + rows, c0:c0 + cols]
                                                      + gt_ref[0][:, c0:c0 + cols] * res)


def _oproj_call(o, w_o, x, gate):
    b, s, d = x.shape
    tm = min(s, 512)
    return pl.pallas_call(
        _oproj_kernel,
        grid=(b, s // tm),
        in_specs=[pl.BlockSpec((1, tm, d), lambda bi, si: (bi, si, 0)),
                  pl.BlockSpec((d, d), lambda bi, si: (0, 0)),
                  pl.BlockSpec((1, tm, d), lambda bi, si: (bi, si, 0)),
                  pl.BlockSpec((1, 1, d), lambda bi, si: (bi, 0, 0))],
        out_specs=pl.BlockSpec((1, tm, d), lambda bi, si: (bi, si, 0)),
        out_shape=jax.ShapeDtypeStruct((b, s, d), F32),
        compiler_params=_params("arbitrary", "arbitrary"),
        name="attn_out_proj",
    )(o, w_o.astype(BF16), x, gate)


def _split_bf16(a):
    hi = a.astype(BF16)
    return hi, (a - hi.astype(F32)).astype(BF16)


def _router_kernel(x_ref, sh_ref, sc_ref, g_ref, wr_ref, br_ref, h_ref, route_ref, idx_ref, cnt_ref,
                   *, ts, n_groups, n_exp):
    first = jnp.logical_and(pl.program_id(0) == 0, pl.program_id(1) == 0)

    @pl.when(first)
    def _():
        cnt_ref[...] = jnp.zeros_like(cnt_ref)

    h = _modulate(x_ref[0], g_ref[...], sh_ref[0], sc_ref[0])
    h_ref[0] = h

    h_hi, h_lo = _split_bf16(h)
    hi_part = jnp.dot(h_hi, wr_ref[...], preferred_element_type=F32)
    lo_part = jnp.dot(h_lo, wr_ref[:, 0:LANES], preferred_element_type=F32)
    logits = (hi_part[:, 0:LANES] + hi_part[:, LANES:2 * LANES] + lo_part) + br_ref[...]

    lane = lax.broadcasted_iota(I32, (ts, LANES), 1)
    neg = jnp.float32(-jnp.inf)

    def first_argmax(vals):
        m = jnp.max(vals, axis=-1, keepdims=True)
        idx = jnp.min(jnp.where(vals == m, lane, LANES), axis=-1, keepdims=True)
        return m, idx

    is_grp = lane < n_groups
    g_max, g_idx = first_argmax(jnp.where(is_grp, logits, neg))
    g_sum = jnp.sum(jnp.where(is_grp, jnp.exp(logits - g_max), 0.0), axis=-1, keepdims=True)
    p_grp = 1.0 / g_sum
    lo = n_groups + g_idx * n_exp
    in_grp = jnp.logical_and(lane >= lo, lane < lo + n_exp)
    sel = jnp.where(in_grp, logits, neg)
    v1, i1 = first_argmax(sel)
    v2, i2 = first_argmax(jnp.where(lane == i1, neg, sel))
    e21 = jnp.exp(v2 - v1)
    w1 = p_grp * (1.0 / (1.0 + e21))
    w2 = p_grp * (e21 / (1.0 + e21))

    oh1 = (lane == i1).astype(F32)
    oh2 = (lane == i2).astype(F32)
    both = oh1 + oh2
    earlier = (lax.broadcasted_iota(I32, (ts, ts), 1) < lax.broadcasted_iota(I32, (ts, ts), 0))
    ranks = jnp.dot(earlier.astype(BF16), both.astype(BF16), preferred_element_type=F32) + cnt_ref[...]
    r1 = jnp.sum(oh1 * ranks, axis=-1, keepdims=True)
    r2 = jnp.sum(oh2 * ranks, axis=-1, keepdims=True)
    cnt_ref[...] = cnt_ref[...] + jnp.sum(both, axis=0, keepdims=True)

    out = jnp.zeros((ts, LANES), F32)
    for k, val in enumerate(((i1 - n_groups).astype(F32), (i2 - n_groups).astype(F32), w1, w2, r1, r2)):
        out = jnp.where(lane == k, val, out)
    route_ref[0] = out
    idx_ref[0] = out.T[0:ROUTE_FIELDS, :].astype(I32)


def _router_call(x, shift, scale, norm_g, w_grp, b_grp, w_exp, b_exp):
    b, s, d = x.shape
    n_groups, n_exp = b_exp.shape
    n_logits = n_groups + n_groups * n_exp
    ts = min(s, 256)
    tiles = s // ts
    w_r = jnp.concatenate([w_grp, w_exp.reshape(d, n_groups * n_exp),
                           jnp.zeros((d, LANES - n_logits), F32)], axis=1)
    b_r = jnp.concatenate([b_grp, b_exp.reshape(-1), jnp.zeros((LANES - n_logits,), F32)]).reshape(1, LANES)
    w_hi = w_r.astype(BF16)
    w_split = jnp.concatenate([w_hi, (w_r - w_hi.astype(F32)).astype(BF16)], axis=1)
    vec = pl.BlockSpec((1, 1, d), lambda bi, si: (bi, 0, 0))
    return pl.pallas_call(
        functools.partial(_router_kernel, ts=ts, n_groups=n_groups, n_exp=n_exp),
        grid=(b, s // ts),
        in_specs=[pl.BlockSpec((1, ts, d), lambda bi, si: (bi, si, 0)), vec, vec,
                  pl.BlockSpec((1, d), lambda bi, si: (0, 0)),
                  pl.BlockSpec((d, 2 * LANES), lambda bi, si: (0, 0)),
                  pl.BlockSpec((1, LANES), lambda bi, si: (0, 0))],
        out_specs=[pl.BlockSpec((1, ts, d), lambda bi, si: (bi, si, 0)),
                   pl.BlockSpec((1, ts, LANES), lambda bi, si: (bi, si, 0)),
                   pl.BlockSpec((1, ROUTE_FIELDS, ts), lambda bi, si: (bi * tiles + si, 0, 0)),
                   pl.BlockSpec((1, LANES), lambda bi, si: (0, 0))],
        out_shape=[jax.ShapeDtypeStruct((b, s, d), F32),
                   jax.ShapeDtypeStruct((b, s, LANES), F32),
                   jax.ShapeDtypeStruct((b * tiles, ROUTE_FIELDS, ts), I32),
                   jax.ShapeDtypeStruct((1, LANES), F32)],
        compiler_params=_params("arbitrary", "arbitrary"),
        name="moe_router",
    )(x, shift, scale, norm_g.reshape(1, d), w_split, b_r)


def _row_gather_start(src_hbm, dst, sem, idx_ref, base):
    n_rows = dst.shape[0]

    def body(c, carry):
        for u in range(DMA_ISSUE_UNROLL):
            r = c * DMA_ISSUE_UNROLL + u
            pltpu.make_async_copy(src_hbm.at[pl.ds(idx_ref[base + r], 1)], dst.at[pl.ds(r, 1)], sem).start()
        return carry
    lax.fori_loop(0, n_rows // DMA_ISSUE_UNROLL, body, 0)


def _row_gather_inline(src_hbm, dst, sem, idx_ref, base):
    for r in range(dst.shape[0]):
        pltpu.make_async_copy(src_hbm.at[pl.ds(idx_ref[base + r], 1)], dst.at[pl.ds(r, 1)], sem).start()


def _row_gather_wait(src_hbm, dst, sem):
    pltpu.make_async_copy(src_hbm.at[pl.ds(0, dst.shape[0])], dst, sem).wait()


def _valid_chunks_start(src_hbm, dst, sem, idx_ref, base, n_valid):
    for c0 in range(0, dst.shape[0], GATHER_CHUNK):
        @pl.when(n_valid > c0)
        def _():
            _row_gather_inline(src_hbm, dst.at[pl.ds(c0, GATHER_CHUNK)], sem, idx_ref, base + c0)


def _valid_chunks_wait(src_hbm, dst, sem, n_valid):
    for c0 in range(0, dst.shape[0], GATHER_CHUNK):
        @pl.when(n_valid > c0)
        def _():
            _row_gather_wait(src_hbm, dst.at[pl.ds(c0, GATHER_CHUNK)], sem)


def _expert_kernel(pos_ref, off_ref, cnt_ref, valid_ref, nact_ref, h_hbm, wg_ref, wu_ref, wd_ref, y_hbm,
                   xbuf, ybuf, gsem, ysem, tok_ref, wg_bf, wu_bf, wd_bf, *, tm, n_tok, n_experts, n_tiles):
    expert = pl.program_id(0)
    n_active = nact_ref[0]
    tiles_e = (cnt_ref[expert] + tm - 1) // tm
    first_tile = off_ref[expert] // tm

    def y_write(slot, g):
        return pltpu.make_async_copy(ybuf.at[slot], y_hbm.at[pl.ds(g * tm, tm)], ysem.at[slot])

    def gather_start(slot, g):
        tile_id = jnp.minimum(g, n_active - 1)
        _valid_chunks_start(h_hbm, xbuf.at[slot], gsem.at[slot], tok_ref, tile_id * tm, valid_ref[tile_id])

    def gather_wait(slot, g):
        _valid_chunks_wait(h_hbm, xbuf.at[slot], gsem.at[slot], valid_ref[jnp.minimum(g, n_active - 1)])

    @pl.when(expert == 0)
    def _():
        def pad_expert(e, carry):
            first = off_ref[e] + cnt_ref[e]
            last = off_ref[e] + ((cnt_ref[e] + tm - 1) // tm) * tm

            def pad(p, c):
                tok_ref[p] = 0
                return c
            return lax.fori_loop(first, last, pad, carry)
        lax.fori_loop(0, n_experts, pad_expert, 0)

        def place(c, carry):
            toks = [c * DMA_ISSUE_UNROLL + u for u in range(DMA_ISSUE_UNROLL)]
            rows = [(pos_ref[i], pos_ref[n_tok + i]) for i in toks]
            for i, (a, b) in zip(toks, rows):
                tok_ref[a] = i
                tok_ref[b] = i
            return carry
        lax.fori_loop(0, n_tok // DMA_ISSUE_UNROLL, place, 0)
        xbuf[...] = jnp.zeros(xbuf.shape, F32)
        for k in range(GATHER_SLOTS - 1):
            gather_start(k, k)

    def tile(g, cast_weights):
        xs = g % GATHER_SLOTS
        ys = g % 2
        gather_wait(xs, g)
        gather_start((g + GATHER_SLOTS - 1) % GATHER_SLOTS, g + GATHER_SLOTS - 1)
        x = xbuf[xs].astype(BF16)
        if cast_weights:
            wg_bf[...] = wg_ref[0].astype(BF16)
        a = jnp.dot(x, wg_bf[...], preferred_element_type=F32)
        if cast_weights:
            wu_bf[...] = wu_ref[0].astype(BF16)
        u = jnp.dot(x, wu_bf[...], preferred_element_type=F32)
        if cast_weights:
            wd_bf[...] = wd_ref[0].astype(BF16)
        hid = (a * (1.0 / (1.0 + jnp.exp(-a)))) * u

        @pl.when(g >= 2)
        def _():
            y_write(ys, g).wait()

        ybuf[ys] = jnp.dot(hid.astype(BF16), wd_bf[...], preferred_element_type=F32)
        y_write(ys, g).start()

    @pl.when(tiles_e > 0)
    def _():
        tile(first_tile, True)

    def tile_body(i, carry):
        tile(first_tile + i, False)
        return carry
    lax.fori_loop(1, tiles_e, tile_body, 0)

    @pl.when(expert == n_experts - 1)
    def _():
        @pl.when(n_active >= 2)
        def _():
            y_write(n_active % 2, 0).wait()

        y_write((n_active + 1) % 2, 0).wait()
        for k in range(GATHER_SLOTS - 1):
            gather_wait((n_active + k) % GATHER_SLOTS, n_active - 1)

        xbuf[0] = jnp.zeros(xbuf.shape[1:], F32)

        def zero_tile(g, carry):
            pltpu.make_async_copy(xbuf.at[0], y_hbm.at[pl.ds(g * tm, tm)], gsem.at[0]).start()
            return carry

        def zero_wait(g, carry):
            pltpu.make_async_copy(xbuf.at[0], y_hbm.at[pl.ds(0, tm)], gsem.at[0]).wait()
            return carry
        lax.fori_loop(n_active, n_tiles, zero_tile, 0)
        lax.fori_loop(n_active, n_tiles, zero_wait, 0)


def _expert_call(h2, pos, offset, cnt, tile_valid, n_active, w_gate, w_up, w_down, layer, tm):
    t, d = h2.shape
    n_all, _, f = w_gate.shape
    n_tiles = tile_valid.shape[0]
    experts_per_layer = n_all // layer[1]
    base = layer[0] * experts_per_layer

    def w_map(e, *_):
        return (base + e, 0, 0)

    grid_spec = pltpu.PrefetchScalarGridSpec(
        num_scalar_prefetch=5,
        grid=(experts_per_layer,),
        in_specs=[pl.BlockSpec(memory_space=pl.ANY),
                  pl.BlockSpec((1, d, f), w_map),
                  pl.BlockSpec((1, d, f), w_map),
                  pl.BlockSpec((1, f, d), w_map)],
        out_specs=pl.BlockSpec(memory_space=pl.ANY),
        scratch_shapes=[pltpu.VMEM((GATHER_SLOTS, tm, d), F32),
                        pltpu.VMEM((2, tm, d), F32),
                        pltpu.SemaphoreType.DMA((GATHER_SLOTS,)),
                        pltpu.SemaphoreType.DMA((2,)),
                        pltpu.SMEM((n_tiles * tm,), I32),
                        pltpu.VMEM((d, f), BF16), pltpu.VMEM((d, f), BF16), pltpu.VMEM((f, d), BF16)],
    )
    return pl.pallas_call(
        functools.partial(_expert_kernel, tm=tm, n_tok=t, n_experts=experts_per_layer, n_tiles=n_tiles),
        grid_spec=grid_spec,
        out_shape=jax.ShapeDtypeStruct((n_tiles * tm, d), F32),
        compiler_params=_params("arbitrary"),
        name="moe_experts",
    )(pos, offset, cnt, tile_valid, n_active, h2, w_gate, w_up, w_down)


def _combine_kernel(pos_ref, x_ref, route_ref, gt_ref, y_hbm, o_ref, ybuf, sem, *, ts, tiles_per_seq):
    b = pl.program_id(0)
    s = pl.program_id(1)
    step = b * tiles_per_seq + s
    n_steps = pl.num_programs(0) * tiles_per_seq
    n_tok = n_steps * ts

    @pl.when(step == 0)
    def _():
        _row_gather_start(y_hbm, ybuf.at[0, 0], sem.at[0], pos_ref, 0)
        _row_gather_start(y_hbm, ybuf.at[0, 1], sem.at[0], pos_ref, n_tok)

    slot = step % 2
    nxt = jnp.minimum(step + 1, n_steps - 1) * ts
    _row_gather_inline(y_hbm, ybuf.at[1 - slot, 0], sem.at[1 - slot], pos_ref, nxt)
    _row_gather_inline(y_hbm, ybuf.at[1 - slot, 1], sem.at[1 - slot], pos_ref, n_tok + nxt)
    _row_gather_wait(y_hbm, ybuf.at[slot, 0], sem.at[slot])
    _row_gather_wait(y_hbm, ybuf.at[slot, 1], sem.at[slot])
    route = route_ref[0]
    moe = route[:, 2:3] * ybuf[slot, 0] + route[:, 3:4] * ybuf[slot, 1]
    o_ref[0] = x_ref[0] + gt_ref[0] * moe

    @pl.when(step == n_steps - 1)
    def _():
        _row_gather_wait(y_hbm, ybuf.at[1 - slot, 0], sem.at[1 - slot])
        _row_gather_wait(y_hbm, ybuf.at[1 - slot, 1], sem.at[1 - slot])


def _combine_call(x, route, gate, y, pos):
    b, s, d = x.shape
    ts = min(s, 256)
    tiles_per_seq = s // ts
    grid_spec = pltpu.PrefetchScalarGridSpec(
        num_scalar_prefetch=1,
        grid=(b, tiles_per_seq),
        in_specs=[pl.BlockSpec((1, ts, d), lambda bi, si, pos: (bi, si, 0)),
                  pl.BlockSpec((1, ts, LANES), lambda bi, si, pos: (bi, si, 0)),
                  pl.BlockSpec((1, 1, d), lambda bi, si, pos: (bi, 0, 0)),
                  pl.BlockSpec(memory_space=pl.ANY)],
        out_specs=pl.BlockSpec((1, ts, d), lambda bi, si, pos: (bi, si, 0)),
        scratch_shapes=[pltpu.VMEM((2, 2, ts, d), F32), pltpu.SemaphoreType.DMA((2,))],
    )
    return pl.pallas_call(
        functools.partial(_combine_kernel, ts=ts, tiles_per_seq=tiles_per_seq),
        grid_spec=grid_spec,
        out_shape=jax.ShapeDtypeStruct((b, s, d), F32),
        compiler_params=_params("arbitrary", "arbitrary"),
        name="moe_combine",
    )(pos, x, route, gate, y)


def _moe(x, shift, scale, gate, norm_g, w_grp, b_grp, w_exp, b_exp, w_gate, w_up, w_down, layer):
    b, s, d = x.shape
    t = b * s
    n_groups, n_exp = b_exp.shape
    n_experts = n_groups * n_exp
    tm = min(t, 256)

    h2, route, idx, counts = _router_call(x, shift, scale, norm_g, w_grp, b_grp, w_exp, b_exp)

    cnt = counts[0, n_groups:n_groups + n_experts].astype(I32)
    tiles_e = (cnt + tm - 1) // tm
    tiles_end = jnp.cumsum(tiles_e)
    offset = (tiles_end - tiles_e) * tm
    n_tiles = (2 * t + n_experts * (tm - 1) + tm - 1) // tm
    n_active = tiles_end[-1:]
    tile_ids = jnp.arange(n_tiles, dtype=I32)
    tile_expert = jnp.minimum(jnp.sum((tiles_end[None, :] <= tile_ids[:, None]).astype(I32), axis=1),
                              n_experts - 1)
    of_tile = tile_expert[:, None] == jnp.arange(n_experts, dtype=I32)[None, :]
    first_tile = jnp.sum(jnp.where(of_tile, (tiles_end - tiles_e)[None, :], 0), axis=1)
    rows_left = jnp.sum(jnp.where(of_tile, cnt[None, :], 0), axis=1) - (tile_ids - first_tile) * tm
    tile_valid = jnp.where(tile_ids < n_active[0], jnp.clip(rows_left, 0, tm), 0)
    fields = idx.transpose(1, 0, 2).reshape(ROUTE_FIELDS, t)
    expert = jnp.concatenate([fields[0], fields[1]])
    rank = jnp.concatenate([fields[4], fields[5]])
    is_e = expert[None, :] == jnp.arange(n_experts, dtype=I32)[:, None]
    pos = jnp.sum(jnp.where(is_e, offset[:, None], 0), axis=0) + rank

    y = _expert_call(h2.reshape(t, d), pos, offset, cnt, tile_valid.astype(I32), n_active,
                     w_gate, w_up, w_down, layer, tm)
    return _combine_call(x, route, gate, y, pos)


def kernel(x, c, ada_w, ada_b, norm1_g, norm2_g, pool_w, pool_scale, w_qkv, w_o, q_norm_g, k_norm_g,
           w_grp, b_grp, w_exp, b_exp, w_gate, w_up, w_down):
    depth = ada_w.shape[0]
    b, s, d = x.shape
    n_mixers = 2
    mod = _ada_call(c, ada_w, ada_b).reshape(depth, b, 6, 1, d)
    wshape = w_gate.shape
    w_gate_s = w_gate.reshape((-1,) + wshape[3:])
    w_up_s = w_up.reshape((-1,) + wshape[3:])
    w_down_s = w_down.reshape((-1,) + w_down.shape[3:])
    for i in range(depth):
        shift1, scale1, gate1, shift2, scale2, gate2 = (mod[i, :, k] for k in range(6))
        j = i // n_mixers
        if i % n_mixers == 0:
            x = _pool_call(x, shift1, scale1, gate1, norm1_g[i], pool_w[j], pool_scale[j])
        else:
            qkv = _qkv_call(x, shift1, scale1, norm1_g[i], w_qkv[j], q_norm_g[j], k_norm_g[j])
            o = _attn_call(qkv, d)
            x = _oproj_call(o, w_o[j], x, gate1)
        x = _moe(x, shift2, scale2, gate2, norm2_g[i], w_grp[i], b_grp[i], w_exp[i], b_exp[i],
                 w_gate_s, w_up_s, w_down_s, (i, depth))
    return x
```

```python
import functools

import jax
import jax.numpy as jnp
from jax import lax
from jax.experimental import pallas as pl
from jax.experimental.pallas import tpu as pltpu

F32 = jnp.float32
BF16 = jnp.bfloat16
I32 = jnp.int32

NORM_EPS = 1e-6
POOL_WINDOWS = (2, 4, 8, 16)
MAX_WINDOW = max(POOL_WINDOWS)
HEAD_DIM = 128
LANES = 128
ROUTE_FIELDS = 8
DMA_ISSUE_UNROLL = 8
GATHER_SLOTS = 3
GATHER_CHUNK = 64
VMEM_LIMIT_BYTES = 56 * 1024 * 1024
LOG_F32_ZERO = -110.0


def _params(*sem):
    return pltpu.CompilerParams(dimension_semantics=sem, vmem_limit_bytes=VMEM_LIMIT_BYTES)


def _modulate(x, g, shift, scale):
    y = x * lax.rsqrt(jnp.mean(x * x, axis=-1, keepdims=True) + NORM_EPS)
    return (y * g) * (1.0 + scale) + shift


def _ada_kernel(c_ref, w_ref, b_ref, o_ref):
    c = c_ref[...]
    c_act = c * (1.0 / (1.0 + jnp.exp(-c)))
    o_ref[0] = jnp.dot(c_act.astype(BF16), w_ref[0].astype(BF16),
                       preferred_element_type=F32) + b_ref[0]


def _ada_call(c, ada_w, ada_b):
    depth, d, n = ada_w.shape
    b = c.shape[0]
    tn = min(n, 1024)
    return pl.pallas_call(
        _ada_kernel,
        grid=(depth, n // tn),
        in_specs=[pl.BlockSpec((b, d), lambda l, j: (0, 0)),
                  pl.BlockSpec((1, d, tn), lambda l, j: (l, 0, j)),
                  pl.BlockSpec((1, 1, tn), lambda l, j: (l, 0, j))],
        out_specs=pl.BlockSpec((1, b, tn), lambda l, j: (l, 0, j)),
        out_shape=jax.ShapeDtypeStruct((depth, b, n), F32),
        compiler_params=_params("arbitrary", "arbitrary"),
        name="ada",
    )(c, ada_w, ada_b.reshape(depth, 1, n))


def _pool_kernel(x_ref, sh_ref, sc_ref, gt_ref, g_ref, pw_ref, ps_ref, o_ref, hext_ref, *, ts, dg):
    s = pl.program_id(1)
    x = x_ref[0]
    h = _modulate(x, g_ref[...], sh_ref[0], sc_ref[0])

    @pl.when(s == 0)
    def _():
        hext_ref[0:MAX_WINDOW, :] = jnp.zeros((MAX_WINDOW, h.shape[1]), F32)

    @pl.when(s > 0)
    def _():
        hext_ref[0:MAX_WINDOW, :] = hext_ref[ts:ts + MAX_WINDOW, :]

    hext_ref[MAX_WINDOW:MAX_WINDOW + ts, :] = h
    pos = s * ts + lax.broadcasted_iota(I32, (ts, dg), 0)
    for gi, w in enumerate(POOL_WINDOWS):
        c0 = gi * dg
        acc = h[:, c0:c0 + dg]
        for k in range(1, w):
            acc = acc + hext_ref[MAX_WINDOW - k:MAX_WINDOW - k + ts, c0:c0 + dg]
        count = jnp.minimum(pos + 1, w).astype(F32)
        pooled = acc / count - h[:, c0:c0 + dg]
        mixed = jnp.dot(pooled.astype(BF16), pw_ref[gi], preferred_element_type=F32)
        o_ref[0, :, c0:c0 + dg] = (x[:, c0:c0 + dg]
                                   + gt_ref[0][:, c0:c0 + dg] * (mixed * ps_ref[:, c0:c0 + dg]))


def _pool_call(x, shift, scale, gate, norm_g, pool_w, pool_scale):
    b, s, d = x.shape
    ng, dg, _ = pool_w.shape
    ts = min(s, 256)
    vec = pl.BlockSpec((1, 1, d), lambda bi, si: (bi, 0, 0))
    row = pl.BlockSpec((1, d), lambda bi, si: (0, 0))
    return pl.pallas_call(
        functools.partial(_pool_kernel, ts=ts, dg=dg),
        grid=(b, s // ts),
        in_specs=[pl.BlockSpec((1, ts, d), lambda bi, si: (bi, si, 0)), vec, vec, vec, row,
                  pl.BlockSpec((ng, dg, dg), lambda bi, si: (0, 0, 0)), row],
        out_specs=pl.BlockSpec((1, ts, d), lambda bi, si: (bi, si, 0)),
        out_shape=jax.ShapeDtypeStruct((b, s, d), F32),
        scratch_shapes=[pltpu.VMEM((ts + MAX_WINDOW, d), F32)],
        compiler_params=_params("arbitrary", "arbitrary"),
        name="pool_mixer",
    )(x, shift, scale, gate, norm_g.reshape(1, d), pool_w.astype(BF16), pool_scale.reshape(1, d))


def _qkv_kernel(x_ref, sh_ref, sc_ref, g_ref, w_ref, ng_ref, o_ref, h_ref, res_ref, *, n_norm_tiles, heads):
    j = pl.program_id(2)
    tm = h_ref.shape[0]
    rows = min(tm, 256)
    sub = min(rows, 64)

    @pl.when(j == 0)
    def _():
        h_ref[...] = _modulate(x_ref[0], g_ref[...], sh_ref[0], sc_ref[0]).astype(BF16)

    is_norm = j < n_norm_tiles
    g = ng_ref[0]
    for r0 in range(0, tm, rows):
        res_ref[r0:r0 + rows, :] = jnp.dot(h_ref[r0:r0 + rows, :], w_ref[...], preferred_element_type=F32)
        for hh in range(heads):
            for r1 in range(r0, r0 + rows, sub):
                blk = res_ref[r1:r1 + sub, hh * HEAD_DIM:(hh + 1) * HEAD_DIM]
                inv = lax.rsqrt(jnp.mean(blk * blk, axis=-1, keepdims=True) + NORM_EPS)
                y = jnp.where(is_norm, (blk * inv) * g, blk)
                o_ref[hh, 0, r1:r1 + sub, :] = y.astype(BF16)


def _qkv_call(x, shift, scale, norm_g, w_qkv, q_norm_g, k_norm_g):
    b, s, d = x.shape
    n = w_qkv.shape[1]
    tm = min(s, 512)
    tn = min(d, 1024)
    heads = tn // HEAD_DIM
    tiles_per_part = d // tn
    qk_gain = jnp.stack([q_norm_g * (HEAD_DIM ** -0.5), k_norm_g, jnp.ones_like(k_norm_g)])
    vec = pl.BlockSpec((1, 1, d), lambda bi, si, j: (bi, 0, 0))
    return pl.pallas_call(
        functools.partial(_qkv_kernel, n_norm_tiles=2 * tiles_per_part, heads=heads),
        grid=(b, s // tm, n // tn),
        in_specs=[pl.BlockSpec((1, tm, d), lambda bi, si, j: (bi, si, 0)), vec, vec,
                  pl.BlockSpec((1, d), lambda bi, si, j: (0, 0)),
                  pl.BlockSpec((d, tn), lambda bi, si, j: (0, j)),
                  pl.BlockSpec((1, 1, HEAD_DIM), lambda bi, si, j: (j // tiles_per_part, 0, 0))],
        out_specs=pl.BlockSpec((heads, 1, tm, HEAD_DIM), lambda bi, si, j: (j, bi, si, 0)),
        out_shape=jax.ShapeDtypeStruct((n // HEAD_DIM, b, s, HEAD_DIM), BF16),
        scratch_shapes=[pltpu.VMEM((tm, d), BF16), pltpu.VMEM((tm, tn), F32)],
        compiler_params=_params("arbitrary", "arbitrary", "arbitrary"),
        name="qkv_proj",
    )(x, shift, scale, norm_g.reshape(1, d), w_qkv.astype(BF16), qk_gain.reshape(3, 1, HEAD_DIM))


def _attn_kernel(q_ref, k_ref, v_ref, o_ref, *, tq, heads):
    qi = pl.program_id(2)
    row = lax.broadcasted_iota(I32, (tq, tq), 0)
    col = lax.broadcasted_iota(I32, (tq, tq), 1)
    suffix_sum = (row >= col).astype(BF16)
    causal = col < row

    def scores(q, k):
        return lax.dot_general(q, k, (((1,), (1,)), ((), ())), preferred_element_type=F32)

    def weights(z, r, suffix, mask):
        log_fail = -(jnp.maximum(z, 0.0) + jnp.log(1.0 + jnp.exp(-jnp.abs(z))))
        if mask is not None:
            log_fail = jnp.where(mask, log_fail, 0.0)
        cum = jnp.dot(log_fail.astype(BF16), suffix, preferred_element_type=F32)
        a = jnp.exp(z + cum + r)
        if mask is not None:
            a = jnp.where(mask, a, 0.0)
        return a.astype(BF16), cum[:, 0:1]

    def diagonal(g):
        start = pl.multiple_of(qi * tq, tq)
        k = k_ref[g, 0, pl.ds(start, tq), :]
        v = v_ref[g, 0, pl.ds(start, tq), :]
        a, r_blk = weights(scores(q_ref[g, 0], k), 0.0, suffix_sum, causal)
        return jnp.dot(a, v, preferred_element_type=F32), r_blk

    def block(j, accs, rs):
        start = pl.multiple_of(j * tq, tq)
        new_accs, new_rs = [], []
        for g in range(heads):
            k = k_ref[g, 0, pl.ds(start, tq), :]
            v = v_ref[g, 0, pl.ds(start, tq), :]
            a, r_blk = weights(scores(q_ref[g, 0], k), rs[g], suffix_sum, None)
            new_accs.append(accs[g] + jnp.dot(a, v, preferred_element_type=F32))
            new_rs.append(rs[g] + r_blk)
        return tuple(new_accs), tuple(new_rs)

    def slowest(rs):
        m = jnp.max(rs[0])
        for r in rs[1:]:
            m = jnp.maximum(m, jnp.max(r))
        return m

    accs, rs = zip(*[diagonal(g) for g in range(heads)])

    def cond(carry):
        j, _, _, r_max = carry
        return jnp.logical_and(j >= 0, r_max > LOG_F32_ZERO)

    def body(carry):
        j, accs, rs, _ = carry
        accs, rs = block(j, accs, rs)
        return j - 1, accs, rs, slowest(rs)

    _, accs, _, _ = lax.while_loop(cond, body, (qi - 1, accs, rs, slowest(rs)))
    for g in range(heads):
        o_ref[0, :, g * HEAD_DIM:(g + 1) * HEAD_DIM] = accs[g].astype(o_ref.dtype)


def _attn_call(qkv, d):
    n3h, b, s, _ = qkv.shape
    nh = n3h // 3
    tq = min(s, 256)
    heads = next(h for h in (8, 4, 2, 1) if nh % h == 0)
    ng = nh // heads
    return pl.pallas_call(
        functools.partial(_attn_kernel, tq=tq, heads=heads),
        grid=(b, ng, s // tq),
        in_specs=[pl.BlockSpec((heads, 1, tq, HEAD_DIM), lambda bi, h, qi: (h, bi, qi, 0)),
                  pl.BlockSpec((heads, 1, s, HEAD_DIM), lambda bi, h, qi: (ng + h, bi, 0, 0)),
                  pl.BlockSpec((heads, 1, s, HEAD_DIM), lambda bi, h, qi: (2 * ng + h, bi, 0, 0))],
        out_specs=pl.BlockSpec((1, tq, heads * HEAD_DIM), lambda bi, h, qi: (bi, qi, h)),
        out_shape=jax.ShapeDtypeStruct((b, s, d), BF16),
        compiler_params=_params("arbitrary", "arbitrary", "arbitrary"),
        name="stickbreak_attn",
    )(qkv, qkv, qkv)


def _oproj_kernel(o_ref, w_ref, x_ref, gt_ref, out_ref):
    tm, d = o_ref.shape[1], w_ref.shape[1]
    rows, cols = min(tm, 256), min(d, 512)
    for r0 in range(0, tm, rows):
        for c0 in range(0, d, cols):
            res = jnp.dot(o_ref[0, r0:r0 + rows, :], w_ref[:, c0:c0 + cols], preferred_element_type=F32)
            out_ref[0, r0:r0 + rows, c0:c0 + cols] = (x_ref[0, r0:r0 + rows, c0:c0 + cols]
                                                      + gt_ref[0][:, c0:c0 + cols] * res)


def _oproj_call(o, w_o, x, gate):
    b, s, d = x.shape
    tm = min(s, 512)
    return pl.pallas_call(
        _oproj_kernel,
        grid=(b, s // tm),
        in_specs=[pl.BlockSpec((1, tm, d), lambda bi, si: (bi, si, 0)),
                  pl.BlockSpec((d, d), lambda bi, si: (0, 0)),
                  pl.BlockSpec((1, tm, d), lambda bi, si: (bi, si, 0)),
                  pl.BlockSpec((1, 1, d), lambda bi, si: (bi, 0, 0))],
        out_specs=pl.BlockSpec((1, tm, d), lambda bi, si: (bi, si, 0)),
        out_shape=jax.ShapeDtypeStruct((b, s, d), F32),
        compiler_params=_params("arbitrary", "arbitrary"),
        name="attn_out_proj",
    )(o, w_o.astype(BF16), x, gate)


def _split_bf16(a):
    hi = a.astype(BF16)
    return hi, (a - hi.astype(F32)).astype(BF16)


def _router_kernel(x_ref, sh_ref, sc_ref, g_ref, wr_ref, br_ref, h_ref, route_ref, idx_ref, cnt_ref,
                   *, ts, n_groups, n_exp):
    first = jnp.logical_and(pl.program_id(0) == 0, pl.program_id(1) == 0)

    @pl.when(first)
    def _():
        cnt_ref[...] = jnp.zeros_like(cnt_ref)

    h = _modulate(x_ref[0], g_ref[...], sh_ref[0], sc_ref[0])
    h_ref[0] = h

    h_hi, h_lo = _split_bf16(h)
    hi_part = jnp.dot(h_hi, wr_ref[...], preferred_element_type=F32)
    lo_part = jnp.dot(h_lo, wr_ref[:, 0:LANES], preferred_element_type=F32)
    logits = (hi_part[:, 0:LANES] + hi_part[:, LANES:2 * LANES] + lo_part) + br_ref[...]

    lane = lax.broadcasted_iota(I32, (ts, LANES), 1)
    neg = jnp.float32(-jnp.inf)

    def first_argmax(vals):
        m = jnp.max(vals, axis=-1, keepdims=True)
        idx = jnp.min(jnp.where(vals == m, lane, LANES), axis=-1, keepdims=True)
        return m, idx

    is_grp = lane < n_groups
    g_max, g_idx = first_argmax(jnp.where(is_grp, logits, neg))
    g_sum = jnp.sum(jnp.where(is_grp, jnp.exp(logits - g_max), 0.0), axis=-1, keepdims=True)
    p_grp = 1.0 / g_sum
    lo = n_groups + g_idx * n_exp
    in_grp = jnp.logical_and(lane >= lo, lane < lo + n_exp)
    sel = jnp.where(in_grp, logits, neg)
    v1, i1 = first_argmax(sel)
    v2, i2 = first_argmax(jnp.where(lane == i1, neg, sel))
    e21 = jnp.exp(v2 - v1)
    w1 = p_grp * (1.0 / (1.0 + e21))
    w2 = p_grp * (e21 / (1.0 + e21))

    oh1 = (lane == i1).astype(F32)
    oh2 = (lane == i2).astype(F32)
    both = oh1 + oh2
    earlier = (lax.broadcasted_iota(I32, (ts, ts), 1) < lax.broadcasted_iota(I32, (ts, ts), 0))
    ranks = jnp.dot(earlier.astype(BF16), both.astype(BF16), preferred_element_type=F32) + cnt_ref[...]
    r1 = jnp.sum(oh1 * ranks, axis=-1, keepdims=True)
    r2 = jnp.sum(oh2 * ranks, axis=-1, keepdims=True)
    cnt_ref[...] = cnt_ref[...] + jnp.sum(both, axis=0, keepdims=True)

    out = jnp.zeros((ts, LANES), F32)
    for k, val in enumerate(((i1 - n_groups).astype(F32), (i2 - n_groups).astype(F32), w1, w2, r1, r2)):
        out = jnp.where(lane == k, val, out)
    route_ref[0] = out
    idx_ref[0] = out.T[0:ROUTE_FIELDS, :].astype(I32)


def _router_call(x, shift, scale, norm_g, w_grp, b_grp, w_exp, b_exp):
    b, s, d = x.shape
    n_groups, n_exp = b_exp.shape
    n_logits = n_groups + n_groups * n_exp
    ts = min(s, 256)
    tiles = s // ts
    w_r = jnp.concatenate([w_grp, w_exp.reshape(d, n_groups * n_exp),
                           jnp.zeros((d, LANES - n_logits), F32)], axis=1)
    b_r = jnp.concatenate([b_grp, b_exp.reshape(-1), jnp.zeros((LANES - n_logits,), F32)]).reshape(1, LANES)
    w_hi = w_r.astype(BF16)
    w_split = jnp.concatenate([w_hi, (w_r - w_hi.astype(F32)).astype(BF16)], axis=1)
    vec = pl.BlockSpec((1, 1, d), lambda bi, si: (bi, 0, 0))
    return pl.pallas_call(
        functools.partial(_router_kernel, ts=ts, n_groups=n_groups, n_exp=n_exp),
        grid=(b, s // ts),
        in_specs=[pl.BlockSpec((1, ts, d), lambda bi, si: (bi, si, 0)), vec, vec,
                  pl.BlockSpec((1, d), lambda bi, si: (0, 0)),
                  pl.BlockSpec((d, 2 * LANES), lambda bi, si: (0, 0)),
                  pl.BlockSpec((1, LANES), lambda bi, si: (0, 0))],
        out_specs=[pl.BlockSpec((1, ts, d), lambda bi, si: (bi, si, 0)),
                   pl.BlockSpec((1, ts, LANES), lambda bi, si: (bi, si, 0)),
                   pl.BlockSpec((1, ROUTE_FIELDS, ts), lambda bi, si: (bi * tiles + si, 0, 0)),
                   pl.BlockSpec((1, LANES), lambda bi, si: (0, 0))],
        out_shape=[jax.ShapeDtypeStruct((b, s, d), F32),
                   jax.ShapeDtypeStruct((b, s, LANES), F32),
                   jax.ShapeDtypeStruct((b * tiles, ROUTE_FIELDS, ts), I32),
                   jax.ShapeDtypeStruct((1, LANES), F32)],
        compiler_params=_params("arbitrary", "arbitrary"),
        name="moe_router",
    )(x, shift, scale, norm_g.reshape(1, d), w_split, b_r)


def _row_gather_start(src_hbm, dst, sem, idx_ref, base):
    n_rows = dst.shape[0]

    def body(c, carry):
        for u in range(DMA_ISSUE_UNROLL):
            r = c * DMA_ISSUE_UNROLL + u
            pltpu.make_async_copy(src_hbm.at[pl.ds(idx_ref[base + r], 1)], dst.at[pl.ds(r, 1)], sem).start()
        return carry
    lax.fori_loop(0, n_rows // DMA_ISSUE_UNROLL, body, 0)


def _row_gather_inline(src_hbm, dst, sem, idx_ref, base):
    for r in range(dst.shape[0]):
        pltpu.make_async_copy(src_hbm.at[pl.ds(idx_ref[base + r], 1)], dst.at[pl.ds(r, 1)], sem).start()


def _row_gather_wait(src_hbm, dst, sem):
    pltpu.make_async_copy(src_hbm.at[pl.ds(0, dst.shape[0])], dst, sem).wait()


def _valid_chunks_start(src_hbm, dst, sem, idx_ref, base, n_valid):
    for c0 in range(0, dst.shape[0], GATHER_CHUNK):
        @pl.when(n_valid > c0)
        def _():
            _row_gather_inline(src_hbm, dst.at[pl.ds(c0, GATHER_CHUNK)], sem, idx_ref, base + c0)


def _valid_chunks_wait(src_hbm, dst, sem, n_valid):
    for c0 in range(0, dst.shape[0], GATHER_CHUNK):
        @pl.when(n_valid > c0)
        def _():
            _row_gather_wait(src_hbm, dst.at[pl.ds(c0, GATHER_CHUNK)], sem)


def _expert_kernel(pos_ref, off_ref, cnt_ref, valid_ref, nact_ref, h_hbm, wg_ref, wu_ref, wd_ref, y_hbm,
                   xbuf, ybuf, gsem, ysem, tok_ref, wg_bf, wu_bf, wd_bf, *, tm, n_tok, n_experts, n_tiles):
    expert = pl.program_id(0)
    n_active = nact_ref[0]
    tiles_e = (cnt_ref[expert] + tm - 1) // tm
    first_tile = off_ref[expert] // tm

    def y_write(slot, g):
        return pltpu.make_async_copy(ybuf.at[slot], y_hbm.at[pl.ds(g * tm, tm)], ysem.at[slot])

    def gather_start(slot, g):
        tile_id = jnp.minimum(g, n_active - 1)
        _valid_chunks_start(h_hbm, xbuf.at[slot], gsem.at[slot], tok_ref, tile_id * tm, valid_ref[tile_id])

    def gather_wait(slot, g):
        _valid_chunks_wait(h_hbm, xbuf.at[slot], gsem.at[slot], valid_ref[jnp.minimum(g, n_active - 1)])

    @pl.when(expert == 0)
    def _():
        def pad_expert(e, carry):
            first = off_ref[e] + cnt_ref[e]
            last = off_ref[e] + ((cnt_ref[e] + tm - 1) // tm) * tm

            def pad(p, c):
                tok_ref[p] = lax.rem(p, n_tok)
                return c
            return lax.fori_loop(first, last, pad, carry)
        lax.fori_loop(0, n_experts, pad_expert, 0)

        def place(c, carry):
            toks = [c * DMA_ISSUE_UNROLL + u for u in range(DMA_ISSUE_UNROLL)]
            rows = [(pos_ref[i], pos_ref[n_tok + i]) for i in toks]
            for i, (a, b) in zip(toks, rows):
                tok_ref[a] = i
                tok_ref[b] = i
            return carry
        lax.fori_loop(0, n_tok // DMA_ISSUE_UNROLL, place, 0)
        xbuf[...] = jnp.zeros(xbuf.shape, F32)
        for k in range(GATHER_SLOTS - 1):
            gather_start(k, k)

    def tile(g, cast_weights):
        xs = g % GATHER_SLOTS
        ys = g % 2
        gather_wait(xs, g)
        gather_start((g + GATHER_SLOTS - 1) % GATHER_SLOTS, g + GATHER_SLOTS - 1)
        x = xbuf[xs].astype(BF16)
        if cast_weights:
            wg_bf[...] = wg_ref[0].astype(BF16)
        a = jnp.dot(x, wg_bf[...], preferred_element_type=F32)
        if cast_weights:
            wu_bf[...] = wu_ref[0].astype(BF16)
        u = jnp.dot(x, wu_bf[...], preferred_element_type=F32)
        if cast_weights:
            wd_bf[...] = wd_ref[0].astype(BF16)
        hid = (a * (1.0 / (1.0 + jnp.exp(-a)))) * u

        @pl.when(g >= 2)
        def _():
            y_write(ys, g).wait()

        ybuf[ys] = jnp.dot(hid.astype(BF16), wd_bf[...], preferred_element_type=F32)
        y_write(ys, g).start()

    @pl.when(tiles_e > 0)
    def _():
        tile(first_tile, True)

    def tile_body(i, carry):
        tile(first_tile + i, False)
        return carry
    lax.fori_loop(1, tiles_e, tile_body, 0)

    @pl.when(expert == n_experts - 1)
    def _():
        @pl.when(n_active >= 2)
        def _():
            y_write(n_active % 2, 0).wait()

        y_write((n_active + 1) % 2, 0).wait()
        for k in range(GATHER_SLOTS - 1):
            gather_wait((n_active + k) % GATHER_SLOTS, n_active - 1)

        xbuf[0] = jnp.zeros(xbuf.shape[1:], F32)

        def zero_tile(g, carry):
            pltpu.make_async_copy(xbuf.at[0], y_hbm.at[pl.ds(g * tm, tm)], gsem.at[0]).start()
            return carry

        def zero_wait(g, carry):
            pltpu.make_async_copy(xbuf.at[0], y_hbm.at[pl.ds(0, tm)], gsem.at[0]).wait()
            return carry
        lax.fori_loop(n_active, n_tiles, zero_tile, 0)
        lax.fori_loop(n_active, n_tiles, zero_wait, 0)


def _expert_call(h2, pos, offset, cnt, tile_valid, n_active, w_gate, w_up, w_down, layer, tm):
    t, d = h2.shape
    n_all, _, f = w_gate.shape
    n_tiles = tile_valid.shape[0]
    experts_per_layer = n_all // layer[1]
    base = layer[0] * experts_per_layer

    def w_map(e, *_):
        return (base + e, 0, 0)

    grid_spec = pltpu.PrefetchScalarGridSpec(
        num_scalar_prefetch=5,
        grid=(experts_per_layer,),
        in_specs=[pl.BlockSpec(memory_space=pl.ANY),
                  pl.BlockSpec((1, d, f), w_map),
                  pl.BlockSpec((1, d, f), w_map),
                  pl.BlockSpec((1, f, d), w_map)],
        out_specs=pl.BlockSpec(memory_space=pl.ANY),
        scratch_shapes=[pltpu.VMEM((GATHER_SLOTS, tm, d), F32),
                        pltpu.VMEM((2, tm, d), F32),
                        pltpu.SemaphoreType.DMA((GATHER_SLOTS,)),
                        pltpu.SemaphoreType.DMA((2,)),
                        pltpu.SMEM((n_tiles * tm,), I32),
                        pltpu.VMEM((d, f), BF16), pltpu.VMEM((d, f), BF16), pltpu.VMEM((f, d), BF16)],
    )
    return pl.pallas_call(
        functools.partial(_expert_kernel, tm=tm, n_tok=t, n_experts=experts_per_layer, n_tiles=n_tiles),
        grid_spec=grid_spec,
        out_shape=jax.ShapeDtypeStruct((n_tiles * tm, d), F32),
        compiler_params=_params("arbitrary"),
        name="moe_experts",
    )(pos, offset, cnt, tile_valid, n_active, h2, w_gate, w_up, w_down)


def _combine_kernel(pos_ref, x_ref, route_ref, gt_ref, y_hbm, o_ref, ybuf, sem, *, ts, tiles_per_seq):
    b = pl.program_id(0)
    s = pl.program_id(1)
    step = b * tiles_per_seq + s
    n_steps = pl.num_programs(0) * tiles_per_seq
    n_tok = n_steps * ts

    @pl.when(step == 0)
    def _():
        _row_gather_start(y_hbm, ybuf.at[0, 0], sem.at[0], pos_ref, 0)
        _row_gather_start(y_hbm, ybuf.at[0, 1], sem.at[0], pos_ref, n_tok)

    slot = step % 2
    nxt = jnp.minimum(step + 1, n_steps - 1) * ts
    _row_gather_inline(y_hbm, ybuf.at[1 - slot, 0], sem.at[1 - slot], pos_ref, nxt)
    _row_gather_inline(y_hbm, ybuf.at[1 - slot, 1], sem.at[1 - slot], pos_ref, n_tok + nxt)
    _row_gather_wait(y_hbm, ybuf.at[slot, 0], sem.at[slot])
    _row_gather_wait(y_hbm, ybuf.at[slot, 1], sem.at[slot])
    route = route_ref[0]
    moe = route[:, 2:3] * ybuf[slot, 0] + route[:, 3:4] * ybuf[slot, 1]
    o_ref[0] = x_ref[0] + gt_ref[0] * moe

    @pl.when(step == n_steps - 1)
    def _():
        _row_gather_wait(y_hbm, ybuf.at[1 - slot, 0], sem.at[1 - slot])
        _row_gather_wait(y_hbm, ybuf.at[1 - slot, 1], sem.at[1 - slot])


def _combine_call(x, route, gate, y, pos):
    b, s, d = x.shape
    ts = min(s, 256)
    tiles_per_seq = s // ts
    grid_spec = pltpu.PrefetchScalarGridSpec(
        num_scalar_prefetch=1,
        grid=(b, tiles_per_seq),
        in_specs=[pl.BlockSpec((1, ts, d), lambda bi, si, pos: (bi, si, 0)),
                  pl.BlockSpec((1, ts, LANES), lambda bi, si, pos: (bi, si, 0)),
                  pl.BlockSpec((1, 1, d), lambda bi, si, pos: (bi, 0, 0)),
                  pl.BlockSpec(memory_space=pl.ANY)],
        out_specs=pl.BlockSpec((1, ts, d), lambda bi, si, pos: (bi, si, 0)),
        scratch_shapes=[pltpu.VMEM((2, 2, ts, d), F32), pltpu.SemaphoreType.DMA((2,))],
    )
    return pl.pallas_call(
        functools.partial(_combine_kernel, ts=ts, tiles_per_seq=tiles_per_seq),
        grid_spec=grid_spec,
        out_shape=jax.ShapeDtypeStruct((b, s, d), F32),
        compiler_params=_params("arbitrary", "arbitrary"),
        name="moe_combine",
    )(pos, x, route, gate, y)


def _moe(x, shift, scale, gate, norm_g, w_grp, b_grp, w_exp, b_exp, w_gate, w_up, w_down, layer):
    b, s, d = x.shape
    t = b * s
    n_groups, n_exp = b_exp.shape
    n_experts = n_groups * n_exp
    tm = min(t, 256)

    h2, route, idx, counts = _router_call(x, shift, scale, norm_g, w_grp, b_grp, w_exp, b_exp)

    cnt = counts[0, n_groups:n_groups + n_experts].astype(I32)
    tiles_e = (cnt + tm - 1) // tm
    tiles_end = jnp.cumsum(tiles_e)
    offset = (tiles_end - tiles_e) * tm
    n_tiles = (2 * t + n_experts * (tm - 1) + tm - 1) // tm
    n_active = tiles_end[-1:]
    tile_ids = jnp.arange(n_tiles, dtype=I32)
    tile_expert = jnp.minimum(jnp.sum((tiles_end[None, :] <= tile_ids[:, None]).astype(I32), axis=1),
                              n_experts - 1)
    of_tile = tile_expert[:, None] == jnp.arange(n_experts, dtype=I32)[None, :]
    first_tile = jnp.sum(jnp.where(of_tile, (tiles_end - tiles_e)[None, :], 0), axis=1)
    rows_left = jnp.sum(jnp.where(of_tile, cnt[None, :], 0), axis=1) - (tile_ids - first_tile) * tm
    tile_valid = jnp.where(tile_ids < n_active[0], jnp.clip(rows_left, 0, tm), 0)
    fields = idx.transpose(1, 0, 2).reshape(ROUTE_FIELDS, t)
    expert = jnp.concatenate([fields[0], fields[1]])
    rank = jnp.concatenate([fields[4], fields[5]])
    is_e = expert[None, :] == jnp.arange(n_experts, dtype=I32)[:, None]
    pos = jnp.sum(jnp.where(is_e, offset[:, None], 0), axis=0) + rank

    y = _expert_call(h2.reshape(t, d), pos, offset, cnt, tile_valid.astype(I32), n_active,
                     w_gate, w_up, w_down, layer, tm)
    return _combine_call(x, route, gate, y, pos)


def kernel(x, c, ada_w, ada_b, norm1_g, norm2_g, pool_w, pool_scale, w_qkv, w_o, q_norm_g, k_norm_g,
           w_grp, b_grp, w_exp, b_exp, w_gate, w_up, w_down):
    depth = ada_w.shape[0]
    b, s, d = x.shape
    n_mixers = 2
    mod = _ada_call(c, ada_w, ada_b).reshape(depth, b, 6, 1, d)
    wshape = w_gate.shape
    w_gate_s = w_gate.reshape((-1,) + wshape[3:])
    w_up_s = w_up.reshape((-1,) + wshape[3:])
    w_down_s = w_down.reshape((-1,) + w_down.shape[3:])
    for i in range(depth):
        shift1, scale1, gate1, shift2, scale2, gate2 = (mod[i, :, k] for k in range(6))
        j = i // n_mixers
        if i % n_mixers == 0:
            x = _pool_call(x, shift1, scale1, gate1, norm1_g[i], pool_w[j], pool_scale[j])
        else:
            qkv = _qkv_call(x, shift1, scale1, norm1_g[i], w_qkv[j], q_norm_g[j], k_norm_g[j])
            o = _attn_call(qkv, d)
            x = _oproj_call(o, w_o[j], x, gate1)
        x = _moe(x, shift2, scale2, gate2, norm2_g[i], w_grp[i], b_grp[i], w_exp[i], b_exp[i],
                 w_gate_s, w_up_s, w_down_s, (i, depth))
    return x
```

```python
import functools

import jax
import jax.numpy as jnp
from jax import lax
from jax.experimental import pallas as pl
from jax.experimental.pallas import tpu as pltpu

F32 = jnp.float32
BF16 = jnp.bfloat16
I32 = jnp.int32

NORM_EPS = 1e-6
POOL_WINDOWS = (2, 4, 8, 16)
MAX_WINDOW = max(POOL_WINDOWS)
HEAD_DIM = 128
LANES = 128
ROUTE_FIELDS = 8
DMA_ISSUE_UNROLL = 8
GATHER_SLOTS = 3
GATHER_CHUNK = 32
VMEM_LIMIT_BYTES = 56 * 1024 * 1024
LOG_F32_ZERO = -110.0


def _params(*sem):
    return pltpu.CompilerParams(dimension_semantics=sem, vmem_limit_bytes=VMEM_LIMIT_BYTES)


def _modulate(x, g, shift, scale):
    y = x * lax.rsqrt(jnp.mean(x * x, axis=-1, keepdims=True) + NORM_EPS)
    return (y * g) * (1.0 + scale) + shift


def _ada_kernel(c_ref, w_ref, b_ref, o_ref):
    c = c_ref[...]
    c_act = c * (1.0 / (1.0 + jnp.exp(-c)))
    o_ref[0] = jnp.dot(c_act.astype(BF16), w_ref[0].astype(BF16),
                       preferred_element_type=F32) + b_ref[0]


def _ada_call(c, ada_w, ada_b):
    depth, d, n = ada_w.shape
    b = c.shape[0]
    tn = min(n, 1024)
    return pl.pallas_call(
        _ada_kernel,
        grid=(depth, n // tn),
        in_specs=[pl.BlockSpec((b, d), lambda l, j: (0, 0)),
                  pl.BlockSpec((1, d, tn), lambda l, j: (l, 0, j)),
                  pl.BlockSpec((1, 1, tn), lambda l, j: (l, 0, j))],
        out_specs=pl.BlockSpec((1, b, tn), lambda l, j: (l, 0, j)),
        out_shape=jax.ShapeDtypeStruct((depth, b, n), F32),
        compiler_params=_params("arbitrary", "arbitrary"),
        name="ada",
    )(c, ada_w, ada_b.reshape(depth, 1, n))


def _pool_kernel(x_ref, sh_ref, sc_ref, gt_ref, g_ref, pw_ref, ps_ref, o_ref, hext_ref, *, ts, dg):
    s = pl.program_id(1)
    x = x_ref[0]
    h = _modulate(x, g_ref[...], sh_ref[0], sc_ref[0])

    @pl.when(s == 0)
    def _():
        hext_ref[0:MAX_WINDOW, :] = jnp.zeros((MAX_WINDOW, h.shape[1]), F32)

    @pl.when(s > 0)
    def _():
        hext_ref[0:MAX_WINDOW, :] = hext_ref[ts:ts + MAX_WINDOW, :]

    hext_ref[MAX_WINDOW:MAX_WINDOW + ts, :] = h
    pos = s * ts + lax.broadcasted_iota(I32, (ts, dg), 0)
    for gi, w in enumerate(POOL_WINDOWS):
        c0 = gi * dg
        acc = h[:, c0:c0 + dg]
        for k in range(1, w):
            acc = acc + hext_ref[MAX_WINDOW - k:MAX_WINDOW - k + ts, c0:c0 + dg]
        count = jnp.minimum(pos + 1, w).astype(F32)
        pooled = acc / count - h[:, c0:c0 + dg]
        mixed = jnp.dot(pooled.astype(BF16), pw_ref[gi], preferred_element_type=F32)
        o_ref[0, :, c0:c0 + dg] = (x[:, c0:c0 + dg]
                                   + gt_ref[0][:, c0:c0 + dg] * (mixed * ps_ref[:, c0:c0 + dg]))


def _pool_call(x, shift, scale, gate, norm_g, pool_w, pool_scale):
    b, s, d = x.shape
    ng, dg, _ = pool_w.shape
    ts = min(s, 256)
    vec = pl.BlockSpec((1, 1, d), lambda bi, si: (bi, 0, 0))
    row = pl.BlockSpec((1, d), lambda bi, si: (0, 0))
    return pl.pallas_call(
        functools.partial(_pool_kernel, ts=ts, dg=dg),
        grid=(b, s // ts),
        in_specs=[pl.BlockSpec((1, ts, d), lambda bi, si: (bi, si, 0)), vec, vec, vec, row,
                  pl.BlockSpec((ng, dg, dg), lambda bi, si: (0, 0, 0)), row],
        out_specs=pl.BlockSpec((1, ts, d), lambda bi, si: (bi, si, 0)),
        out_shape=jax.ShapeDtypeStruct((b, s, d), F32),
        scratch_shapes=[pltpu.VMEM((ts + MAX_WINDOW, d), F32)],
        compiler_params=_params("arbitrary", "arbitrary"),
        name="pool_mixer",
    )(x, shift, scale, gate, norm_g.reshape(1, d), pool_w.astype(BF16), pool_scale.reshape(1, d))


def _qkv_kernel(x_ref, sh_ref, sc_ref, g_ref, w_ref, ng_ref, o_ref, h_ref, res_ref, *, n_norm_tiles, heads):
    j = pl.program_id(2)
    tm = h_ref.shape[0]
    rows = min(tm, 256)
    sub = min(rows, 64)

    @pl.when(j == 0)
    def _():
        h_ref[...] = _modulate(x_ref[0], g_ref[...], sh_ref[0], sc_ref[0]).astype(BF16)

    is_norm = j < n_norm_tiles
    g = ng_ref[0]
    for r0 in range(0, tm, rows):
        res_ref[r0:r0 + rows, :] = jnp.dot(h_ref[r0:r0 + rows, :], w_ref[...], preferred_element_type=F32)
        for hh in range(heads):
            for r1 in range(r0, r0 + rows, sub):
                blk = res_ref[r1:r1 + sub, hh * HEAD_DIM:(hh + 1) * HEAD_DIM]
                inv = lax.rsqrt(jnp.mean(blk * blk, axis=-1, keepdims=True) + NORM_EPS)
                y = jnp.where(is_norm, (blk * inv) * g, blk)
                o_ref[hh, 0, r1:r1 + sub, :] = y.astype(BF16)


def _qkv_call(x, shift, scale, norm_g, w_qkv, q_norm_g, k_norm_g):
    b, s, d = x.shape
    n = w_qkv.shape[1]
    tm = min(s, 1024)
    tn = min(d, 1024)
    heads = tn // HEAD_DIM
    tiles_per_part = d // tn
    qk_gain = jnp.stack([q_norm_g * (HEAD_DIM ** -0.5), k_norm_g, jnp.ones_like(k_norm_g)])
    vec = pl.BlockSpec((1, 1, d), lambda bi, si, j: (bi, 0, 0))
    return pl.pallas_call(
        functools.partial(_qkv_kernel, n_norm_tiles=2 * tiles_per_part, heads=heads),
        grid=(b, s // tm, n // tn),
        in_specs=[pl.BlockSpec((1, tm, d), lambda bi, si, j: (bi, si, 0)), vec, vec,
                  pl.BlockSpec((1, d), lambda bi, si, j: (0, 0)),
                  pl.BlockSpec((d, tn), lambda bi, si, j: (0, j)),
                  pl.BlockSpec((1, 1, HEAD_DIM), lambda bi, si, j: (j // tiles_per_part, 0, 0))],
        out_specs=pl.BlockSpec((heads, 1, tm, HEAD_DIM), lambda bi, si, j: (j, bi, si, 0)),
        out_shape=jax.ShapeDtypeStruct((n // HEAD_DIM, b, s, HEAD_DIM), BF16),
        scratch_shapes=[pltpu.VMEM((tm, d), BF16), pltpu.VMEM((tm, tn), F32)],
        compiler_params=_params("arbitrary", "arbitrary", "arbitrary"),
        name="qkv_proj",
    )(x, shift, scale, norm_g.reshape(1, d), w_qkv.astype(BF16), qk_gain.reshape(3, 1, HEAD_DIM))


def _attn_kernel(q_ref, k_ref, v_ref, o_ref, *, tq, heads):
    qi = pl.program_id(2)
    row = lax.broadcasted_iota(I32, (tq, tq), 0)
    col = lax.broadcasted_iota(I32, (tq, tq), 1)
    suffix_sum = (row >= col).astype(BF16)
    causal = col < row

    def scores(q, k):
        return lax.dot_general(q, k, (((1,), (1,)), ((), ())), preferred_element_type=F32)

    def weights(z, r, suffix, mask):
        log_fail = -(jnp.maximum(z, 0.0) + jnp.log(1.0 + jnp.exp(-jnp.abs(z))))
        if mask is not None:
            log_fail = jnp.where(mask, log_fail, 0.0)
        cum = jnp.dot(log_fail.astype(BF16), suffix, preferred_element_type=F32)
        a = jnp.exp(z + cum + r)
        if mask is not None:
            a = jnp.where(mask, a, 0.0)
        return a.astype(BF16), cum[:, 0:1]

    def diagonal(g):
        start = pl.multiple_of(qi * tq, tq)
        k = k_ref[g, 0, pl.ds(start, tq), :]
        v = v_ref[g, 0, pl.ds(start, tq), :]
        a, r_blk = weights(scores(q_ref[g, 0], k), 0.0, suffix_sum, causal)
        return jnp.dot(a, v, preferred_element_type=F32), r_blk

    def block(j, accs, rs):
        start = pl.multiple_of(j * tq, tq)
        new_accs, new_rs = [], []
        for g in range(heads):
            k = k_ref[g, 0, pl.ds(start, tq), :]
            v = v_ref[g, 0, pl.ds(start, tq), :]
            a, r_blk = weights(scores(q_ref[g, 0], k), rs[g], suffix_sum, None)
            new_accs.append(accs[g] + jnp.dot(a, v, preferred_element_type=F32))
            new_rs.append(rs[g] + r_blk)
        return tuple(new_accs), tuple(new_rs)

    def slowest(rs):
        m = jnp.max(rs[0])
        for r in rs[1:]:
            m = jnp.maximum(m, jnp.max(r))
        return m

    accs, rs = zip(*[diagonal(g) for g in range(heads)])

    def cond(carry):
        j, _, _, r_max = carry
        return jnp.logical_and(j >= 0, r_max > LOG_F32_ZERO)

    def body(carry):
        j, accs, rs, _ = carry
        accs, rs = block(j, accs, rs)
        return j - 1, accs, rs, slowest(rs)

    _, accs, _, _ = lax.while_loop(cond, body, (qi - 1, accs, rs, slowest(rs)))
    for g in range(heads):
        o_ref[0, :, g * HEAD_DIM:(g + 1) * HEAD_DIM] = accs[g].astype(o_ref.dtype)


def _attn_call(qkv, d):
    n3h, b, s, _ = qkv.shape
    nh = n3h // 3
    tq = min(s, 256)
    heads = next(h for h in (8, 4, 2, 1) if nh % h == 0)
    ng = nh // heads
    return pl.pallas_call(
        functools.partial(_attn_kernel, tq=tq, heads=heads),
        grid=(b, ng, s // tq),
        in_specs=[pl.BlockSpec((heads, 1, tq, HEAD_DIM), lambda bi, h, qi: (h, bi, qi, 0)),
                  pl.BlockSpec((heads, 1, s, HEAD_DIM), lambda bi, h, qi: (ng + h, bi, 0, 0)),
                  pl.BlockSpec((heads, 1, s, HEAD_DIM), lambda bi, h, qi: (2 * ng + h, bi, 0, 0))],
        out_specs=pl.BlockSpec((1, tq, heads * HEAD_DIM), lambda bi, h, qi: (bi, qi, h)),
        out_shape=jax.ShapeDtypeStruct((b, s, d), BF16),
        compiler_params=_params("arbitrary", "arbitrary", "arbitrary"),
        name="stickbreak_attn",
    )(qkv, qkv, qkv)


def _oproj_kernel(o_ref, w_ref, x_ref, gt_ref, out_ref):
    tm, d = o_ref.shape[1], w_ref.shape[1]
    rows, cols = min(tm, 256), min(d, 512)
    for r0 in range(0, tm, rows):
        for c0 in range(0, d, cols):
            res = jnp.dot(o_ref[0, r0:r0 + rows, :], w_ref[:, c0:c0 + cols], preferred_element_type=F32)
            out_ref[0, r0:r0 + rows, c0:c0 + cols] = (x_ref[0, r0:r0 + rows, c0:c0 + cols]
                                                      + gt_ref[0][:, c0:c0 + cols] * res)


def _oproj_call(o, w_o, x, gate):
    b, s, d = x.shape
    tm = min(s, 512)
    return pl.pallas_call(
        _oproj_kernel,
        grid=(b, s // tm),
        in_specs=[pl.BlockSpec((1, tm, d), lambda bi, si: (bi, si, 0)),
                  pl.BlockSpec((d, d), lambda bi, si: (0, 0)),
                  pl.BlockSpec((1, tm, d), lambda bi, si: (bi, si, 0)),
                  pl.BlockSpec((1, 1, d), lambda bi, si: (bi, 0, 0))],
        out_specs=pl.BlockSpec((1, tm, d), lambda bi, si: (bi, si, 0)),
        out_shape=jax.ShapeDtypeStruct((b, s, d), F32),
        compiler_params=_params("arbitrary", "arbitrary"),
        name="attn_out_proj",
    )(o, w_o.astype(BF16), x, gate)


def _split_bf16(a):
    hi = a.astype(BF16)
    return hi, (a - hi.astype(F32)).astype(BF16)


def _router_kernel(x_ref, sh_ref, sc_ref, g_ref, wr_ref, br_ref, h_ref, route_ref, idx_ref, cnt_ref,
                   *, ts, n_groups, n_exp):
    first = jnp.logical_and(pl.program_id(0) == 0, pl.program_id(1) == 0)

    @pl.when(first)
    def _():
        cnt_ref[...] = jnp.zeros_like(cnt_ref)

    h = _modulate(x_ref[0], g_ref[...], sh_ref[0], sc_ref[0])
    h_ref[0] = h

    h_hi, h_lo = _split_bf16(h)
    hi_part = jnp.dot(h_hi, wr_ref[...], preferred_element_type=F32)
    lo_part = jnp.dot(h_lo, wr_ref[:, 0:LANES], preferred_element_type=F32)
    logits = (hi_part[:, 0:LANES] + hi_part[:, LANES:2 * LANES] + lo_part) + br_ref[...]

    lane = lax.broadcasted_iota(I32, (ts, LANES), 1)
    neg = jnp.float32(-jnp.inf)

    def first_argmax(vals):
        m = jnp.max(vals, axis=-1, keepdims=True)
        idx = jnp.min(jnp.where(vals == m, lane, LANES), axis=-1, keepdims=True)
        return m, idx

    is_grp = lane < n_groups
    g_max, g_idx = first_argmax(jnp.where(is_grp, logits, neg))
    g_sum = jnp.sum(jnp.where(is_grp, jnp.exp(logits - g_max), 0.0), axis=-1, keepdims=True)
    p_grp = 1.0 / g_sum
    lo = n_groups + g_idx * n_exp
    in_grp = jnp.logical_and(lane >= lo, lane < lo + n_exp)
    sel = jnp.where(in_grp, logits, neg)
    v1, i1 = first_argmax(sel)
    v2, i2 = first_argmax(jnp.where(lane == i1, neg, sel))
    e21 = jnp.exp(v2 - v1)
    w1 = p_grp * (1.0 / (1.0 + e21))
    w2 = p_grp * (e21 / (1.0 + e21))

    oh1 = (lane == i1).astype(F32)
    oh2 = (lane == i2).astype(F32)
    both = oh1 + oh2
    earlier = (lax.broadcasted_iota(I32, (ts, ts), 1) < lax.broadcasted_iota(I32, (ts, ts), 0))
    ranks = jnp.dot(earlier.astype(BF16), both.astype(BF16), preferred_element_type=F32) + cnt_ref[...]
    r1 = jnp.sum(oh1 * ranks, axis=-1, keepdims=True)
    r2 = jnp.sum(oh2 * ranks, axis=-1, keepdims=True)
    cnt_ref[...] = cnt_ref[...] + jnp.sum(both, axis=0, keepdims=True)

    out = jnp.zeros((ts, LANES), F32)
    for k, val in enumerate(((i1 - n_groups).astype(F32), (i2 - n_groups).astype(F32), w1, w2, r1, r2)):
        out = jnp.where(lane == k, val, out)
    route_ref[0] = out
    idx_ref[0] = out.T[0:ROUTE_FIELDS, :].astype(I32)


def _router_call(x, shift, scale, norm_g, w_grp, b_grp, w_exp, b_exp):
    b, s, d = x.shape
    n_groups, n_exp = b_exp.shape
    n_logits = n_groups + n_groups * n_exp
    ts = min(s, 256)
    tiles = s // ts
    w_r = jnp.concatenate([w_grp, w_exp.reshape(d, n_groups * n_exp),
                           jnp.zeros((d, LANES - n_logits), F32)], axis=1)
    b_r = jnp.concatenate([b_grp, b_exp.reshape(-1), jnp.zeros((LANES - n_logits,), F32)]).reshape(1, LANES)
    w_hi = w_r.astype(BF16)
    w_split = jnp.concatenate([w_hi, (w_r - w_hi.astype(F32)).astype(BF16)], axis=1)
    vec = pl.BlockSpec((1, 1, d), lambda bi, si: (bi, 0, 0))
    return pl.pallas_call(
        functools.partial(_router_kernel, ts=ts, n_groups=n_groups, n_exp=n_exp),
        grid=(b, s // ts),
        in_specs=[pl.BlockSpec((1, ts, d), lambda bi, si: (bi, si, 0)), vec, vec,
                  pl.BlockSpec((1, d), lambda bi, si: (0, 0)),
                  pl.BlockSpec((d, 2 * LANES), lambda bi, si: (0, 0)),
                  pl.BlockSpec((1, LANES), lambda bi, si: (0, 0))],
        out_specs=[pl.BlockSpec((1, ts, d), lambda bi, si: (bi, si, 0)),
                   pl.BlockSpec((1, ts, LANES), lambda bi, si: (bi, si, 0)),
                   pl.BlockSpec((1, ROUTE_FIELDS, ts), lambda bi, si: (bi * tiles + si, 0, 0)),
                   pl.BlockSpec((1, LANES), lambda bi, si: (0, 0))],
        out_shape=[jax.ShapeDtypeStruct((b, s, d), F32),
                   jax.ShapeDtypeStruct((b, s, LANES), F32),
                   jax.ShapeDtypeStruct((b * tiles, ROUTE_FIELDS, ts), I32),
                   jax.ShapeDtypeStruct((1, LANES), F32)],
        compiler_params=_params("arbitrary", "arbitrary"),
        name="moe_router",
    )(x, shift, scale, norm_g.reshape(1, d), w_split, b_r)


def _row_gather_start(src_hbm, dst, sem, idx_ref, base):
    n_rows = dst.shape[0]

    def body(c, carry):
        for u in range(DMA_ISSUE_UNROLL):
            r = c * DMA_ISSUE_UNROLL + u
            pltpu.make_async_copy(src_hbm.at[pl.ds(idx_ref[base + r], 1)], dst.at[pl.ds(r, 1)], sem).start()
        return carry
    lax.fori_loop(0, n_rows // DMA_ISSUE_UNROLL, body, 0)


def _row_gather_inline(src_hbm, dst, sem, idx_ref, base):
    for r in range(dst.shape[0]):
        pltpu.make_async_copy(src_hbm.at[pl.ds(idx_ref[base + r], 1)], dst.at[pl.ds(r, 1)], sem).start()


def _row_gather_wait(src_hbm, dst, sem):
    pltpu.make_async_copy(src_hbm.at[pl.ds(0, dst.shape[0])], dst, sem).wait()


def _valid_chunks_start(src_hbm, dst, sem, idx_ref, base, n_valid):
    for c0 in range(0, dst.shape[0], GATHER_CHUNK):
        @pl.when(n_valid > c0)
        def _():
            _row_gather_inline(src_hbm, dst.at[pl.ds(c0, GATHER_CHUNK)], sem, idx_ref, base + c0)


def _valid_chunks_wait(src_hbm, dst, sem, n_valid):
    for c0 in range(0, dst.shape[0], GATHER_CHUNK):
        @pl.when(n_valid > c0)
        def _():
            _row_gather_wait(src_hbm, dst.at[pl.ds(c0, GATHER_CHUNK)], sem)


def _expert_kernel(pos_ref, off_ref, cnt_ref, valid_ref, nact_ref, h_hbm, wg_ref, wu_ref, wd_ref, y_hbm,
                   xbuf, ybuf, gsem, ysem, tok_ref, wg_bf, wu_bf, wd_bf, *, tm, n_tok, n_experts, n_tiles):
    expert = pl.program_id(0)
    n_active = nact_ref[0]
    tiles_e = (cnt_ref[expert] + tm - 1) // tm
    first_tile = off_ref[expert] // tm

    def y_write(slot, g):
        return pltpu.make_async_copy(ybuf.at[slot], y_hbm.at[pl.ds(g * tm, tm)], ysem.at[slot])

    def gather_start(slot, g):
        tile_id = jnp.minimum(g, n_active - 1)
        _valid_chunks_start(h_hbm, xbuf.at[slot], gsem.at[slot], tok_ref, tile_id * tm, valid_ref[tile_id])

    def gather_wait(slot, g):
        _valid_chunks_wait(h_hbm, xbuf.at[slot], gsem.at[slot], valid_ref[jnp.minimum(g, n_active - 1)])

    @pl.when(expert == 0)
    def _():
        def pad_expert(e, carry):
            first = off_ref[e] + cnt_ref[e]
            last = off_ref[e] + ((cnt_ref[e] + tm - 1) // tm) * tm

            def pad(p, c):
                tok_ref[p] = lax.rem(p, n_tok)
                return c
            return lax.fori_loop(first, last, pad, carry)
        lax.fori_loop(0, n_experts, pad_expert, 0)

        def place(c, carry):
            toks = [c * DMA_ISSUE_UNROLL + u for u in range(DMA_ISSUE_UNROLL)]
            rows = [(pos_ref[i], pos_ref[n_tok + i]) for i in toks]
            for i, (a, b) in zip(toks, rows):
                tok_ref[a] = i
                tok_ref[b] = i
            return carry
        lax.fori_loop(0, n_tok // DMA_ISSUE_UNROLL, place, 0)
        xbuf[...] = jnp.zeros(xbuf.shape, F32)
        for k in range(GATHER_SLOTS - 1):
            gather_start(k, k)

    def tile(g, cast_weights):
        xs = g % GATHER_SLOTS
        ys = g % 2
        gather_wait(xs, g)
        gather_start((g + GATHER_SLOTS - 1) % GATHER_SLOTS, g + GATHER_SLOTS - 1)
        x = xbuf[xs].astype(BF16)
        if cast_weights:
            wg_bf[...] = wg_ref[0].astype(BF16)
        a = jnp.dot(x, wg_bf[...], preferred_element_type=F32)
        if cast_weights:
            wu_bf[...] = wu_ref[0].astype(BF16)
        u = jnp.dot(x, wu_bf[...], preferred_element_type=F32)
        if cast_weights:
            wd_bf[...] = wd_ref[0].astype(BF16)
        hid = (a * (1.0 / (1.0 + jnp.exp(-a)))) * u

        @pl.when(g >= 2)
        def _():
            y_write(ys, g).wait()

        ybuf[ys] = jnp.dot(hid.astype(BF16), wd_bf[...], preferred_element_type=F32)
        y_write(ys, g).start()

    @pl.when(tiles_e > 0)
    def _():
        tile(first_tile, True)

    def tile_body(i, carry):
        tile(first_tile + i, False)
        return carry
    lax.fori_loop(1, tiles_e, tile_body, 0)

    @pl.when(expert == n_experts - 1)
    def _():
        @pl.when(n_active >= 2)
        def _():
            y_write(n_active % 2, 0).wait()

        y_write((n_active + 1) % 2, 0).wait()
        for k in range(GATHER_SLOTS - 1):
            gather_wait((n_active + k) % GATHER_SLOTS, n_active - 1)

        xbuf[0] = jnp.zeros(xbuf.shape[1:], F32)

        def zero_tile(g, carry):
            pltpu.make_async_copy(xbuf.at[0], y_hbm.at[pl.ds(g * tm, tm)], gsem.at[0]).start()
            return carry

        def zero_wait(g, carry):
            pltpu.make_async_copy(xbuf.at[0], y_hbm.at[pl.ds(0, tm)], gsem.at[0]).wait()
            return carry
        lax.fori_loop(n_active, n_tiles, zero_tile, 0)
        lax.fori_loop(n_active, n_tiles, zero_wait, 0)


def _expert_call(h2, pos, offset, cnt, tile_valid, n_active, w_gate, w_up, w_down, layer, tm):
    t, d = h2.shape
    n_all, _, f = w_gate.shape
    n_tiles = tile_valid.shape[0]
    experts_per_layer = n_all // layer[1]
    base = layer[0] * experts_per_layer

    def w_map(e, *_):
        return (base + e, 0, 0)

    grid_spec = pltpu.PrefetchScalarGridSpec(
        num_scalar_prefetch=5,
        grid=(experts_per_layer,),
        in_specs=[pl.BlockSpec(memory_space=pl.ANY),
                  pl.BlockSpec((1, d, f), w_map),
                  pl.BlockSpec((1, d, f), w_map),
                  pl.BlockSpec((1, f, d), w_map)],
        out_specs=pl.BlockSpec(memory_space=pl.ANY),
        scratch_shapes=[pltpu.VMEM((GATHER_SLOTS, tm, d), F32),
                        pltpu.VMEM((2, tm, d), F32),
                        pltpu.SemaphoreType.DMA((GATHER_SLOTS,)),
                        pltpu.SemaphoreType.DMA((2,)),
                        pltpu.SMEM((n_tiles * tm,), I32),
                        pltpu.VMEM((d, f), BF16), pltpu.VMEM((d, f), BF16), pltpu.VMEM((f, d), BF16)],
    )
    return pl.pallas_call(
        functools.partial(_expert_kernel, tm=tm, n_tok=t, n_experts=experts_per_layer, n_tiles=n_tiles),
        grid_spec=grid_spec,
        out_shape=jax.ShapeDtypeStruct((n_tiles * tm, d), F32),
        compiler_params=_params("arbitrary"),
        name="moe_experts",
    )(pos, offset, cnt, tile_valid, n_active, h2, w_gate, w_up, w_down)


def _combine_kernel(pos_ref, x_ref, route_ref, gt_ref, y_hbm, o_ref, ybuf, sem, *, ts, tiles_per_seq):
    b = pl.program_id(0)
    s = pl.program_id(1)
    step = b * tiles_per_seq + s
    n_steps = pl.num_programs(0) * tiles_per_seq
    n_tok = n_steps * ts

    @pl.when(step == 0)
    def _():
        _row_gather_start(y_hbm, ybuf.at[0, 0], sem.at[0], pos_ref, 0)
        _row_gather_start(y_hbm, ybuf.at[0, 1], sem.at[0], pos_ref, n_tok)

    slot = step % 2
    nxt = jnp.minimum(step + 1, n_steps - 1) * ts
    _row_gather_inline(y_hbm, ybuf.at[1 - slot, 0], sem.at[1 - slot], pos_ref, nxt)
    _row_gather_inline(y_hbm, ybuf.at[1 - slot, 1], sem.at[1 - slot], pos_ref, n_tok + nxt)
    _row_gather_wait(y_hbm, ybuf.at[slot, 0], sem.at[slot])
    _row_gather_wait(y_hbm, ybuf.at[slot, 1], sem.at[slot])
    route = route_ref[0]
    moe = route[:, 2:3] * ybuf[slot, 0] + route[:, 3:4] * ybuf[slot, 1]
    o_ref[0] = x_ref[0] + gt_ref[0] * moe

    @pl.when(step == n_steps - 1)
    def _():
        _row_gather_wait(y_hbm, ybuf.at[1 - slot, 0], sem.at[1 - slot])
        _row_gather_wait(y_hbm, ybuf.at[1 - slot, 1], sem.at[1 - slot])


def _combine_call(x, route, gate, y, pos):
    b, s, d = x.shape
    ts = min(s, 256)
    tiles_per_seq = s // ts
    grid_spec = pltpu.PrefetchScalarGridSpec(
        num_scalar_prefetch=1,
        grid=(b, tiles_per_seq),
        in_specs=[pl.BlockSpec((1, ts, d), lambda bi, si, pos: (bi, si, 0)),
                  pl.BlockSpec((1, ts, LANES), lambda bi, si, pos: (bi, si, 0)),
                  pl.BlockSpec((1, 1, d), lambda bi, si, pos: (bi, 0, 0)),
                  pl.BlockSpec(memory_space=pl.ANY)],
        out_specs=pl.BlockSpec((1, ts, d), lambda bi, si, pos: (bi, si, 0)),
        scratch_shapes=[pltpu.VMEM((2, 2, ts, d), F32), pltpu.SemaphoreType.DMA((2,))],
    )
    return pl.pallas_call(
        functools.partial(_combine_kernel, ts=ts, tiles_per_seq=tiles_per_seq),
        grid_spec=grid_spec,
        out_shape=jax.ShapeDtypeStruct((b, s, d), F32),
        compiler_params=_params("arbitrary", "arbitrary"),
        name="moe_combine",
    )(pos, x, route, gate, y)


def _moe(x, shift, scale, gate, norm_g, w_grp, b_grp, w_exp, b_exp, w_gate, w_up, w_down, layer):
    b, s, d = x.shape
    t = b * s
    n_groups, n_exp = b_exp.shape
    n_experts = n_groups * n_exp
    tm = min(t, 256)

    h2, route, idx, counts = _router_call(x, shift, scale, norm_g, w_grp, b_grp, w_exp, b_exp)

    cnt = counts[0, n_groups:n_groups + n_experts].astype(I32)
    tiles_e = (cnt + tm - 1) // tm
    tiles_end = jnp.cumsum(tiles_e)
    offset = (tiles_end - tiles_e) * tm
    n_tiles = (2 * t + n_experts * (tm - 1) + tm - 1) // tm
    n_active = tiles_end[-1:]
    tile_ids = jnp.arange(n_tiles, dtype=I32)
    tile_expert = jnp.minimum(jnp.sum((tiles_end[None, :] <= tile_ids[:, None]).astype(I32), axis=1),
                              n_experts - 1)
    of_tile = tile_expert[:, None] == jnp.arange(n_experts, dtype=I32)[None, :]
    first_tile = jnp.sum(jnp.where(of_tile, (tiles_end - tiles_e)[None, :], 0), axis=1)
    rows_left = jnp.sum(jnp.where(of_tile, cnt[None, :], 0), axis=1) - (tile_ids - first_tile) * tm
    tile_valid = jnp.where(tile_ids < n_active[0], jnp.clip(rows_left, 0, tm), 0)
    fields = idx.transpose(1, 0, 2).reshape(ROUTE_FIELDS, t)
    expert = jnp.concatenate([fields[0], fields[1]])
    rank = jnp.concatenate([fields[4], fields[5]])
    is_e = expert[None, :] == jnp.arange(n_experts, dtype=I32)[:, None]
    pos = jnp.sum(jnp.where(is_e, offset[:, None], 0), axis=0) + rank

    y = _expert_call(h2.reshape(t, d), pos, offset, cnt, tile_valid.astype(I32), n_active,
                     w_gate, w_up, w_down, layer, tm)
    return _combine_call(x, route, gate, y, pos)


def kernel(x, c, ada_w, ada_b, norm1_g, norm2_g, pool_w, pool_scale, w_qkv, w_o, q_norm_g, k_norm_g,
           w_grp, b_grp, w_exp, b_exp, w_gate, w_up, w_down):
    depth = ada_w.shape[0]
    b, s, d = x.shape
    n_mixers = 2
    mod = _ada_call(c, ada_w, ada_b).reshape(depth, b, 6, 1, d)
    wshape = w_gate.shape
    w_gate_s = w_gate.reshape((-1,) + wshape[3:])
    w_up_s = w_up.reshape((-1,) + wshape[3:])
    w_down_s = w_down.reshape((-1,) + w_down.shape[3:])
    for i in range(depth):
        shift1, scale1, gate1, shift2, scale2, gate2 = (mod[i, :, k] for k in range(6))
        j = i // n_mixers
        if i % n_mixers == 0:
            x = _pool_call(x, shift1, scale1, gate1, norm1_g[i], pool_w[j], pool_scale[j])
        else:
            qkv = _qkv_call(x, shift1, scale1, norm1_g[i], w_qkv[j], q_norm_g[j], k_norm_g[j])
            o = _attn_call(qkv, d)
            x = _oproj_call(o, w_o[j], x, gate1)
        x = _moe(x, shift2, scale2, gate2, norm2_g[i], w_grp[i], b_grp[i], w_exp[i], b_exp[i],
                 w_gate_s, w_up_s, w_down_s, (i, depth))
    return x
```

```python
import functools

import jax
import jax.numpy as jnp
from jax import lax
from jax.experimental import pallas as pl
from jax.experimental.pallas import tpu as pltpu

F32 = jnp.float32
BF16 = jnp.bfloat16
I32 = jnp.int32

NORM_EPS = 1e-6
POOL_WINDOWS = (2, 4, 8, 16)
MAX_WINDOW = max(POOL_WINDOWS)
HEAD_DIM = 128
LANES = 128
ROUTE_FIELDS = 8
DMA_ISSUE_UNROLL = 8
GATHER_SLOTS = 3
GATHER_CHUNK = 32
VMEM_LIMIT_BYTES = 56 * 1024 * 1024
LOG_F32_ZERO = -110.0


def _params(*sem):
    return pltpu.CompilerParams(dimension_semantics=sem, vmem_limit_bytes=VMEM_LIMIT_BYTES)


def _modulate(x, g, shift, scale):
    y = x * lax.rsqrt(jnp.mean(x * x, axis=-1, keepdims=True) + NORM_EPS)
    return (y * g) * (1.0 + scale) + shift


def _ada_kernel(c_ref, w_ref, b_ref, o_ref):
    c = c_ref[...]
    c_act = c * (1.0 / (1.0 + jnp.exp(-c)))
    o_ref[0] = jnp.dot(c_act.astype(BF16), w_ref[0].astype(BF16),
                       preferred_element_type=F32) + b_ref[0]


def _ada_call(c, ada_w, ada_b):
    depth, d, n = ada_w.shape
    b = c.shape[0]
    tn = min(n, 1024)
    return pl.pallas_call(
        _ada_kernel,
        grid=(depth, n // tn),
        in_specs=[pl.BlockSpec((b, d), lambda l, j: (0, 0)),
                  pl.BlockSpec((1, d, tn), lambda l, j: (l, 0, j)),
                  pl.BlockSpec((1, 1, tn), lambda l, j: (l, 0, j))],
        out_specs=pl.BlockSpec((1, b, tn), lambda l, j: (l, 0, j)),
        out_shape=jax.ShapeDtypeStruct((depth, b, n), F32),
        compiler_params=_params("arbitrary", "arbitrary"),
        name="ada",
    )(c, ada_w, ada_b.reshape(depth, 1, n))


def _pool_kernel(x_ref, sh_ref, sc_ref, gt_ref, g_ref, pw_ref, ps_ref, o_ref, hext_ref, *, ts, dg):
    s = pl.program_id(1)
    x = x_ref[0]
    h = _modulate(x, g_ref[...], sh_ref[0], sc_ref[0])

    @pl.when(s == 0)
    def _():
        hext_ref[0:MAX_WINDOW, :] = jnp.zeros((MAX_WINDOW, h.shape[1]), F32)

    @pl.when(s > 0)
    def _():
        hext_ref[0:MAX_WINDOW, :] = hext_ref[ts:ts + MAX_WINDOW, :]

    hext_ref[MAX_WINDOW:MAX_WINDOW + ts, :] = h
    pos = s * ts + lax.broadcasted_iota(I32, (ts, dg), 0)
    for gi, w in enumerate(POOL_WINDOWS):
        c0 = gi * dg
        acc = h[:, c0:c0 + dg]
        for k in range(1, w):
            acc = acc + hext_ref[MAX_WINDOW - k:MAX_WINDOW - k + ts, c0:c0 + dg]
        count = jnp.minimum(pos + 1, w).astype(F32)
        pooled = acc / count - h[:, c0:c0 + dg]
        mixed = jnp.dot(pooled.astype(BF16), pw_ref[gi], preferred_element_type=F32)
        o_ref[0, :, c0:c0 + dg] = (x[:, c0:c0 + dg]
                                   + gt_ref[0][:, c0:c0 + dg] * (mixed * ps_ref[:, c0:c0 + dg]))


def _pool_call(x, shift, scale, gate, norm_g, pool_w, pool_scale):
    b, s, d = x.shape
    ng, dg, _ = pool_w.shape
    ts = min(s, 256)
    vec = pl.BlockSpec((1, 1, d), lambda bi, si: (bi, 0, 0))
    row = pl.BlockSpec((1, d), lambda bi, si: (0, 0))
    return pl.pallas_call(
        functools.partial(_pool_kernel, ts=ts, dg=dg),
        grid=(b, s // ts),
        in_specs=[pl.BlockSpec((1, ts, d), lambda bi, si: (bi, si, 0)), vec, vec, vec, row,
                  pl.BlockSpec((ng, dg, dg), lambda bi, si: (0, 0, 0)), row],
        out_specs=pl.BlockSpec((1, ts, d), lambda bi, si: (bi, si, 0)),
        out_shape=jax.ShapeDtypeStruct((b, s, d), F32),
        scratch_shapes=[pltpu.VMEM((ts + MAX_WINDOW, d), F32)],
        compiler_params=_params("arbitrary", "arbitrary"),
        name="pool_mixer",
    )(x, shift, scale, gate, norm_g.reshape(1, d), pool_w.astype(BF16), pool_scale.reshape(1, d))


def _qkv_kernel(x_ref, sh_ref, sc_ref, g_ref, w_ref, ng_ref, o_ref, h_ref, res_ref, *, n_norm_tiles, heads):
    j = pl.program_id(2)
    tm = h_ref.shape[0]
    rows = min(tm, 256)
    sub = min(rows, 64)

    @pl.when(j == 0)
    def _():
        h_ref[...] = _modulate(x_ref[0], g_ref[...], sh_ref[0], sc_ref[0]).astype(BF16)

    is_norm = j < n_norm_tiles
    g = ng_ref[0]
    for r0 in range(0, tm, rows):
        res_ref[r0:r0 + rows, :] = jnp.dot(h_ref[r0:r0 + rows, :], w_ref[...], preferred_element_type=F32)
        for hh in range(heads):
            for r1 in range(r0, r0 + rows, sub):
                blk = res_ref[r1:r1 + sub, hh * HEAD_DIM:(hh + 1) * HEAD_DIM]
                inv = lax.rsqrt(jnp.mean(blk * blk, axis=-1, keepdims=True) + NORM_EPS)
                y = jnp.where(is_norm, (blk * inv) * g, blk)
                o_ref[hh, 0, r1:r1 + sub, :] = y.astype(BF16)


def _qkv_call(x, shift, scale, norm_g, w_qkv, q_norm_g, k_norm_g):
    b, s, d = x.shape
    n = w_qkv.shape[1]
    tm = min(s, 1024)
    tn = min(d, 1024)
    heads = tn // HEAD_DIM
    tiles_per_part = d // tn
    qk_gain = jnp.stack([q_norm_g * (HEAD_DIM ** -0.5), k_norm_g, jnp.ones_like(k_norm_g)])
    vec = pl.BlockSpec((1, 1, d), lambda bi, si, j: (bi, 0, 0))
    return pl.pallas_call(
        functools.partial(_qkv_kernel, n_norm_tiles=2 * tiles_per_part, heads=heads),
        grid=(b, s // tm, n // tn),
        in_specs=[pl.BlockSpec((1, tm, d), lambda bi, si, j: (bi, si, 0)), vec, vec,
                  pl.BlockSpec((1, d), lambda bi, si, j: (0, 0)),
                  pl.BlockSpec((d, tn), lambda bi, si, j: (0, j)),
                  pl.BlockSpec((1, 1, HEAD_DIM), lambda bi, si, j: (j // tiles_per_part, 0, 0))],
        out_specs=pl.BlockSpec((heads, 1, tm, HEAD_DIM), lambda bi, si, j: (j, bi, si, 0)),
        out_shape=jax.ShapeDtypeStruct((n // HEAD_DIM, b, s, HEAD_DIM), BF16),
        scratch_shapes=[pltpu.VMEM((tm, d), BF16), pltpu.VMEM((tm, tn), F32)],
        compiler_params=_params("arbitrary", "arbitrary", "arbitrary"),
        name="qkv_proj",
    )(x, shift, scale, norm_g.reshape(1, d), w_qkv.astype(BF16), qk_gain.reshape(3, 1, HEAD_DIM))


def _attn_kernel(q_ref, k_ref, v_ref, o_ref, *, tq, heads):
    qi = pl.program_id(2)
    row = lax.broadcasted_iota(I32, (tq, tq), 0)
    col = lax.broadcasted_iota(I32, (tq, tq), 1)
    suffix_sum = (row >= col).astype(BF16)
    causal = col < row

    def scores(q, k):
        return lax.dot_general(q, k, (((1,), (1,)), ((), ())), preferred_element_type=F32)

    def weights(z, r, suffix, mask):
        log_fail = -(jnp.maximum(z, 0.0) + jnp.log(1.0 + jnp.exp(-jnp.abs(z))))
        if mask is not None:
            log_fail = jnp.where(mask, log_fail, 0.0)
        cum = jnp.dot(log_fail.astype(BF16), suffix, preferred_element_type=F32)
        a = jnp.exp(z + cum + r)
        if mask is not None:
            a = jnp.where(mask, a, 0.0)
        return a.astype(BF16), cum[:, 0:1]

    def diagonal(g):
        start = pl.multiple_of(qi * tq, tq)
        k = k_ref[g, 0, pl.ds(start, tq), :]
        v = v_ref[g, 0, pl.ds(start, tq), :]
        a, r_blk = weights(scores(q_ref[g, 0], k), 0.0, suffix_sum, causal)
        return jnp.dot(a, v, preferred_element_type=F32), r_blk

    def block(j, accs, rs):
        start = pl.multiple_of(j * tq, tq)
        new_accs, new_rs = [], []
        for g in range(heads):
            k = k_ref[g, 0, pl.ds(start, tq), :]
            v = v_ref[g, 0, pl.ds(start, tq), :]
            a, r_blk = weights(scores(q_ref[g, 0], k), rs[g], suffix_sum, None)
            new_accs.append(accs[g] + jnp.dot(a, v, preferred_element_type=F32))
            new_rs.append(rs[g] + r_blk)
        return tuple(new_accs), tuple(new_rs)

    def slowest(rs):
        m = jnp.max(rs[0])
        for r in rs[1:]:
            m = jnp.maximum(m, jnp.max(r))
        return m

    accs, rs = zip(*[diagonal(g) for g in range(heads)])

    def cond(carry):
        j, _, _, r_max = carry
        return jnp.logical_and(j >= 0, r_max > LOG_F32_ZERO)

    def body(carry):
        j, accs, rs, _ = carry
        accs, rs = block(j, accs, rs)
        return j - 1, accs, rs, slowest(rs)

    _, accs, _, _ = lax.while_loop(cond, body, (qi - 1, accs, rs, slowest(rs)))
    for g in range(heads):
        o_ref[0, :, g * HEAD_DIM:(g + 1) * HEAD_DIM] = accs[g].astype(o_ref.dtype)


def _attn_call(qkv, d):
    n3h, b, s, _ = qkv.shape
    nh = n3h // 3
    tq = min(s, 256)
    heads = next(h for h in (8, 4, 2, 1) if nh % h == 0)
    ng = nh // heads
    return pl.pallas_call(
        functools.partial(_attn_kernel, tq=tq, heads=heads),
        grid=(b, ng, s // tq),
        in_specs=[pl.BlockSpec((heads, 1, tq, HEAD_DIM), lambda bi, h, qi: (h, bi, qi, 0)),
                  pl.BlockSpec((heads, 1, s, HEAD_DIM), lambda bi, h, qi: (ng + h, bi, 0, 0)),
                  pl.BlockSpec((heads, 1, s, HEAD_DIM), lambda bi, h, qi: (2 * ng + h, bi, 0, 0))],
        out_specs=pl.BlockSpec((1, tq, heads * HEAD_DIM), lambda bi, h, qi: (bi, qi, h)),
        out_shape=jax.ShapeDtypeStruct((b, s, d), BF16),
        compiler_params=_params("arbitrary", "arbitrary", "arbitrary"),
        name="stickbreak_attn",
    )(qkv, qkv, qkv)


def _oproj_kernel(o_ref, w_ref, x_ref, gt_ref, out_ref):
    tm, d = o_ref.shape[1], w_ref.shape[1]
    rows, cols = min(tm, 256), min(d, 512)
    for r0 in range(0, tm, rows):
        for c0 in range(0, d, cols):
            res = jnp.dot(o_ref[0, r0:r0 + rows, :], w_ref[:, c0:c0 + cols], preferred_element_type=F32)
            out_ref[0, r0:r0 + rows, c0:c0 + cols] = (x_ref[0, r0:r0 + rows, c0:c0 + cols]
                                                      + gt_ref[0][:, c0:c0 + cols] * res)


def _oproj_call(o, w_o, x, gate):
    b, s, d = x.shape
    tm = min(s, 512)
    return pl.pallas_call(
        _oproj_kernel,
        grid=(b, s // tm),
        in_specs=[pl.BlockSpec((1, tm, d), lambda bi, si: (bi, si, 0)),
                  pl.BlockSpec((d, d), lambda bi, si: (0, 0)),
                  pl.BlockSpec((1, tm, d), lambda bi, si: (bi, si, 0)),
                  pl.BlockSpec((1, 1, d), lambda bi, si: (bi, 0, 0))],
        out_specs=pl.BlockSpec((1, tm, d), lambda bi, si: (bi, si, 0)),
        out_shape=jax.ShapeDtypeStruct((b, s, d), F32),
        compiler_params=_params("arbitrary", "arbitrary"),
        name="attn_out_proj",
    )(o, w_o.astype(BF16), x, gate)


def _split_bf16(a):
    hi = a.astype(BF16)
    return hi, (a - hi.astype(F32)).astype(BF16)


def _router_kernel(x_ref, sh_ref, sc_ref, g_ref, wr_ref, br_ref, h_ref, route_ref, idx_ref, cnt_ref,
                   *, ts, n_groups, n_exp):
    first = jnp.logical_and(pl.program_id(0) == 0, pl.program_id(1) == 0)

    @pl.when(first)
    def _():
        cnt_ref[...] = jnp.zeros_like(cnt_ref)

    h = _modulate(x_ref[0], g_ref[...], sh_ref[0], sc_ref[0])
    h_ref[0] = h

    h_hi, h_lo = _split_bf16(h)
    hi_part = jnp.dot(h_hi, wr_ref[...], preferred_element_type=F32)
    lo_part = jnp.dot(h_lo, wr_ref[:, 0:LANES], preferred_element_type=F32)
    logits = (hi_part[:, 0:LANES] + hi_part[:, LANES:2 * LANES] + lo_part) + br_ref[...]

    lane = lax.broadcasted_iota(I32, (ts, LANES), 1)
    neg = jnp.float32(-jnp.inf)

    def first_argmax(vals):
        m = jnp.max(vals, axis=-1, keepdims=True)
        idx = jnp.min(jnp.where(vals == m, lane, LANES), axis=-1, keepdims=True)
        return m, idx

    is_grp = lane < n_groups
    g_max, g_idx = first_argmax(jnp.where(is_grp, logits, neg))
    g_sum = jnp.sum(jnp.where(is_grp, jnp.exp(logits - g_max), 0.0), axis=-1, keepdims=True)
    p_grp = 1.0 / g_sum
    lo = n_groups + g_idx * n_exp
    in_grp = jnp.logical_and(lane >= lo, lane < lo + n_exp)
    sel = jnp.where(in_grp, logits, neg)
    v1, i1 = first_argmax(sel)
    v2, i2 = first_argmax(jnp.where(lane == i1, neg, sel))
    e21 = jnp.exp(v2 - v1)
    w1 = p_grp * (1.0 / (1.0 + e21))
    w2 = p_grp * (e21 / (1.0 + e21))

    oh1 = (lane == i1).astype(F32)
    oh2 = (lane == i2).astype(F32)
    both = oh1 + oh2
    earlier = (lax.broadcasted_iota(I32, (ts, ts), 1) < lax.broadcasted_iota(I32, (ts, ts), 0))
    ranks = jnp.dot(earlier.astype(BF16), both.astype(BF16), preferred_element_type=F32) + cnt_ref[...]
    r1 = jnp.sum(oh1 * ranks, axis=-1, keepdims=True)
    r2 = jnp.sum(oh2 * ranks, axis=-1, keepdims=True)
    cnt_ref[...] = cnt_ref[...] + jnp.sum(both, axis=0, keepdims=True)

    out = jnp.zeros((ts, LANES), F32)
    for k, val in enumerate(((i1 - n_groups).astype(F32), (i2 - n_groups).astype(F32), w1, w2, r1, r2)):
        out = jnp.where(lane == k, val, out)
    route_ref[0] = out
    idx_ref[0] = out.T[0:ROUTE_FIELDS, :].astype(I32)


def _router_call(x, shift, scale, norm_g, w_grp, b_grp, w_exp, b_exp):
    b, s, d = x.shape
    n_groups, n_exp = b_exp.shape
    n_logits = n_groups + n_groups * n_exp
    ts = min(s, 256)
    tiles = s // ts
    w_r = jnp.concatenate([w_grp, w_exp.reshape(d, n_groups * n_exp),
                           jnp.zeros((d, LANES - n_logits), F32)], axis=1)
    b_r = jnp.concatenate([b_grp, b_exp.reshape(-1), jnp.zeros((LANES - n_logits,), F32)]).reshape(1, LANES)
    w_hi = w_r.astype(BF16)
    w_split = jnp.concatenate([w_hi, (w_r - w_hi.astype(F32)).astype(BF16)], axis=1)
    vec = pl.BlockSpec((1, 1, d), lambda bi, si: (bi, 0, 0))
    return pl.pallas_call(
        functools.partial(_router_kernel, ts=ts, n_groups=n_groups, n_exp=n_exp),
        grid=(b, s // ts),
        in_specs=[pl.BlockSpec((1, ts, d), lambda bi, si: (bi, si, 0)), vec, vec,
                  pl.BlockSpec((1, d), lambda bi, si: (0, 0)),
                  pl.BlockSpec((d, 2 * LANES), lambda bi, si: (0, 0)),
                  pl.BlockSpec((1, LANES), lambda bi, si: (0, 0))],
        out_specs=[pl.BlockSpec((1, ts, d), lambda bi, si: (bi, si, 0)),
                   pl.BlockSpec((1, ts, LANES), lambda bi, si: (bi, si, 0)),
                   pl.BlockSpec((1, ROUTE_FIELDS, ts), lambda bi, si: (bi * tiles + si, 0, 0)),
                   pl.BlockSpec((1, LANES), lambda bi, si: (0, 0))],
        out_shape=[jax.ShapeDtypeStruct((b, s, d), F32),
                   jax.ShapeDtypeStruct((b, s, LANES), F32),
                   jax.ShapeDtypeStruct((b * tiles, ROUTE_FIELDS, ts), I32),
                   jax.ShapeDtypeStruct((1, LANES), F32)],
        compiler_params=_params("arbitrary", "arbitrary"),
        name="moe_router",
    )(x, shift, scale, norm_g.reshape(1, d), w_split, b_r)


def _row_gather_start(src_hbm, dst, sem, idx_ref, base):
    n_rows = dst.shape[0]

    def body(c, carry):
        for u in range(DMA_ISSUE_UNROLL):
            r = c * DMA_ISSUE_UNROLL + u
            pltpu.make_async_copy(src_hbm.at[pl.ds(idx_ref[base + r], 1)], dst.at[pl.ds(r, 1)], sem).start()
        return carry
    lax.fori_loop(0, n_rows // DMA_ISSUE_UNROLL, body, 0)


def _row_gather_inline(src_hbm, dst, sem, idx_ref, base, dma_threads=1):
    for r in range(dst.shape[0]):
        pltpu.make_async_copy(src_hbm.at[pl.ds(idx_ref[base + r], 1)], dst.at[pl.ds(r, 1)],
                              sem).start(priority=r % dma_threads)


def _row_gather_wait(src_hbm, dst, sem):
    pltpu.make_async_copy(src_hbm.at[pl.ds(0, dst.shape[0])], dst, sem).wait()


def _valid_chunks_start(src_hbm, dst, sem, idx_ref, base, n_valid):
    for c0 in range(0, dst.shape[0], GATHER_CHUNK):
        @pl.when(n_valid > c0)
        def _():
            _row_gather_inline(src_hbm, dst.at[pl.ds(c0, GATHER_CHUNK)], sem, idx_ref, base + c0)


def _valid_chunks_wait(src_hbm, dst, sem, n_valid):
    for c0 in range(0, dst.shape[0], GATHER_CHUNK):
        @pl.when(n_valid > c0)
        def _():
            _row_gather_wait(src_hbm, dst.at[pl.ds(c0, GATHER_CHUNK)], sem)


def _expert_kernel(pos_ref, off_ref, cnt_ref, valid_ref, nact_ref, h_hbm, wg_ref, wu_ref, wd_ref, y_hbm,
                   xbuf, ybuf, gsem, ysem, tok_ref, wg_bf, wu_bf, wd_bf, *, tm, n_tok, n_experts, n_tiles):
    expert = pl.program_id(0)
    n_active = nact_ref[0]
    tiles_e = (cnt_ref[expert] + tm - 1) // tm
    first_tile = off_ref[expert] // tm

    def y_write(slot, g):
        return pltpu.make_async_copy(ybuf.at[slot], y_hbm.at[pl.ds(g * tm, tm)], ysem.at[slot])

    def gather_start(slot, g):
        tile_id = jnp.minimum(g, n_active - 1)
        _valid_chunks_start(h_hbm, xbuf.at[slot], gsem.at[slot], tok_ref, tile_id * tm, valid_ref[tile_id])

    def gather_wait(slot, g):
        _valid_chunks_wait(h_hbm, xbuf.at[slot], gsem.at[slot], valid_ref[jnp.minimum(g, n_active - 1)])

    @pl.when(expert == 0)
    def _():
        def pad_expert(e, carry):
            first = off_ref[e] + cnt_ref[e]
            last = off_ref[e] + ((cnt_ref[e] + tm - 1) // tm) * tm

            def pad(p, c):
                tok_ref[p] = lax.rem(p, n_tok)
                return c
            return lax.fori_loop(first, last, pad, carry)
        lax.fori_loop(0, n_experts, pad_expert, 0)

        def place(c, carry):
            toks = [c * DMA_ISSUE_UNROLL + u for u in range(DMA_ISSUE_UNROLL)]
            rows = [(pos_ref[i], pos_ref[n_tok + i]) for i in toks]
            for i, (a, b) in zip(toks, rows):
                tok_ref[a] = i
                tok_ref[b] = i
            return carry
        lax.fori_loop(0, n_tok // DMA_ISSUE_UNROLL, place, 0)
        xbuf[...] = jnp.zeros(xbuf.shape, F32)
        for k in range(GATHER_SLOTS - 1):
            gather_start(k, k)

    def tile(g, cast_weights):
        xs = g % GATHER_SLOTS
        ys = g % 2
        gather_wait(xs, g)
        gather_start((g + GATHER_SLOTS - 1) % GATHER_SLOTS, g + GATHER_SLOTS - 1)
        x = xbuf[xs].astype(BF16)
        if cast_weights:
            wg_bf[...] = wg_ref[0].astype(BF16)
        a = jnp.dot(x, wg_bf[...], preferred_element_type=F32)
        if cast_weights:
            wu_bf[...] = wu_ref[0].astype(BF16)
        u = jnp.dot(x, wu_bf[...], preferred_element_type=F32)
        if cast_weights:
            wd_bf[...] = wd_ref[0].astype(BF16)
        hid = (a * (1.0 / (1.0 + jnp.exp(-a)))) * u

        @pl.when(g >= 2)
        def _():
            y_write(ys, g).wait()

        ybuf[ys] = jnp.dot(hid.astype(BF16), wd_bf[...], preferred_element_type=F32)
        y_write(ys, g).start()

    @pl.when(tiles_e > 0)
    def _():
        tile(first_tile, True)

    def tile_body(i, carry):
        tile(first_tile + i, False)
        return carry
    lax.fori_loop(1, tiles_e, tile_body, 0)

    @pl.when(expert == n_experts - 1)
    def _():
        @pl.when(n_active >= 2)
        def _():
            y_write(n_active % 2, 0).wait()

        y_write((n_active + 1) % 2, 0).wait()
        for k in range(GATHER_SLOTS - 1):
            gather_wait((n_active + k) % GATHER_SLOTS, n_active - 1)

        xbuf[0] = jnp.zeros(xbuf.shape[1:], F32)

        def zero_tile(g, carry):
            pltpu.make_async_copy(xbuf.at[0], y_hbm.at[pl.ds(g * tm, tm)], gsem.at[0]).start()
            return carry

        def zero_wait(g, carry):
            pltpu.make_async_copy(xbuf.at[0], y_hbm.at[pl.ds(0, tm)], gsem.at[0]).wait()
            return carry
        lax.fori_loop(n_active, n_tiles, zero_tile, 0)
        lax.fori_loop(n_active, n_tiles, zero_wait, 0)


def _expert_call(h2, pos, offset, cnt, tile_valid, n_active, w_gate, w_up, w_down, layer, tm):
    t, d = h2.shape
    n_all, _, f = w_gate.shape
    n_tiles = tile_valid.shape[0]
    experts_per_layer = n_all // layer[1]
    base = layer[0] * experts_per_layer

    def w_map(e, *_):
        return (base + e, 0, 0)

    grid_spec = pltpu.PrefetchScalarGridSpec(
        num_scalar_prefetch=5,
        grid=(experts_per_layer,),
        in_specs=[pl.BlockSpec(memory_space=pl.ANY),
                  pl.BlockSpec((1, d, f), w_map),
                  pl.BlockSpec((1, d, f), w_map),
                  pl.BlockSpec((1, f, d), w_map)],
        out_specs=pl.BlockSpec(memory_space=pl.ANY),
        scratch_shapes=[pltpu.VMEM((GATHER_SLOTS, tm, d), F32),
                        pltpu.VMEM((2, tm, d), F32),
                        pltpu.SemaphoreType.DMA((GATHER_SLOTS,)),
                        pltpu.SemaphoreType.DMA((2,)),
                        pltpu.SMEM((n_tiles * tm,), I32),
                        pltpu.VMEM((d, f), BF16), pltpu.VMEM((d, f), BF16), pltpu.VMEM((f, d), BF16)],
    )
    return pl.pallas_call(
        functools.partial(_expert_kernel, tm=tm, n_tok=t, n_experts=experts_per_layer, n_tiles=n_tiles),
        grid_spec=grid_spec,
        out_shape=jax.ShapeDtypeStruct((n_tiles * tm, d), F32),
        compiler_params=_params("arbitrary"),
        name="moe_experts",
    )(pos, offset, cnt, tile_valid, n_active, h2, w_gate, w_up, w_down)


def _combine_kernel(pos_ref, x_ref, route_ref, gt_ref, y_hbm, o_ref, ybuf, sem, *, ts, tiles_per_seq):
    b = pl.program_id(0)
    s = pl.program_id(1)
    step = b * tiles_per_seq + s
    n_steps = pl.num_programs(0) * tiles_per_seq
    n_tok = n_steps * ts

    @pl.when(step == 0)
    def _():
        _row_gather_start(y_hbm, ybuf.at[0, 0], sem.at[0], pos_ref, 0)
        _row_gather_start(y_hbm, ybuf.at[0, 1], sem.at[0], pos_ref, n_tok)

    slot = step % 2
    nxt = jnp.minimum(step + 1, n_steps - 1) * ts
    _row_gather_inline(y_hbm, ybuf.at[1 - slot, 0], sem.at[1 - slot], pos_ref, nxt, dma_threads=2)
    _row_gather_inline(y_hbm, ybuf.at[1 - slot, 1], sem.at[1 - slot], pos_ref, n_tok + nxt, dma_threads=2)
    _row_gather_wait(y_hbm, ybuf.at[slot, 0], sem.at[slot])
    _row_gather_wait(y_hbm, ybuf.at[slot, 1], sem.at[slot])
    route = route_ref[0]
    moe = route[:, 2:3] * ybuf[slot, 0] + route[:, 3:4] * ybuf[slot, 1]
    o_ref[0] = x_ref[0] + gt_ref[0] * moe

    @pl.when(step == n_steps - 1)
    def _():
        _row_gather_wait(y_hbm, ybuf.at[1 - slot, 0], sem.at[1 - slot])
        _row_gather_wait(y_hbm, ybuf.at[1 - slot, 1], sem.at[1 - slot])


def _combine_call(x, route, gate, y, pos):
    b, s, d = x.shape
    ts = min(s, 256)
    tiles_per_seq = s // ts
    grid_spec = pltpu.PrefetchScalarGridSpec(
        num_scalar_prefetch=1,
        grid=(b, tiles_per_seq),
        in_specs=[pl.BlockSpec((1, ts, d), lambda bi, si, pos: (bi, si, 0)),
                  pl.BlockSpec((1, ts, LANES), lambda bi, si, pos: (bi, si, 0)),
                  pl.BlockSpec((1, 1, d), lambda bi, si, pos: (bi, 0, 0)),
                  pl.BlockSpec(memory_space=pl.ANY)],
        out_specs=pl.BlockSpec((1, ts, d), lambda bi, si, pos: (bi, si, 0)),
        scratch_shapes=[pltpu.VMEM((2, 2, ts, d), F32), pltpu.SemaphoreType.DMA((2,))],
    )
    return pl.pallas_call(
        functools.partial(_combine_kernel, ts=ts, tiles_per_seq=tiles_per_seq),
        grid_spec=grid_spec,
        out_shape=jax.ShapeDtypeStruct((b, s, d), F32),
        compiler_params=_params("arbitrary", "arbitrary"),
        name="moe_combine",
    )(pos, x, route, gate, y)


def _moe(x, shift, scale, gate, norm_g, w_grp, b_grp, w_exp, b_exp, w_gate, w_up, w_down, layer):
    b, s, d = x.shape
    t = b * s
    n_groups, n_exp = b_exp.shape
    n_experts = n_groups * n_exp
    tm = min(t, 256)

    h2, route, idx, counts = _router_call(x, shift, scale, norm_g, w_grp, b_grp, w_exp, b_exp)

    cnt = counts[0, n_groups:n_groups + n_experts].astype(I32)
    tiles_e = (cnt + tm - 1) // tm
    tiles_end = jnp.cumsum(tiles_e)
    offset = (tiles_end - tiles_e) * tm
    n_tiles = (2 * t + n_experts * (tm - 1) + tm - 1) // tm
    n_active = tiles_end[-1:]
    tile_ids = jnp.arange(n_tiles, dtype=I32)
    tile_expert = jnp.minimum(jnp.sum((tiles_end[None, :] <= tile_ids[:, None]).astype(I32), axis=1),
                              n_experts - 1)
    of_tile = tile_expert[:, None] == jnp.arange(n_experts, dtype=I32)[None, :]
    first_tile = jnp.sum(jnp.where(of_tile, (tiles_end - tiles_e)[None, :], 0), axis=1)
    rows_left = jnp.sum(jnp.where(of_tile, cnt[None, :], 0), axis=1) - (tile_ids - first_tile) * tm
    tile_valid = jnp.where(tile_ids < n_active[0], jnp.clip(rows_left, 0, tm), 0)
    fields = idx.transpose(1, 0, 2).reshape(ROUTE_FIELDS, t)
    expert = jnp.concatenate([fields[0], fields[1]])
    rank = jnp.concatenate([fields[4], fields[5]])
    is_e = expert[None, :] == jnp.arange(n_experts, dtype=I32)[:, None]
    pos = jnp.sum(jnp.where(is_e, offset[:, None], 0), axis=0) + rank

    y = _expert_call(h2.reshape(t, d), pos, offset, cnt, tile_valid.astype(I32), n_active,
                     w_gate, w_up, w_down, layer, tm)
    return _combine_call(x, route, gate, y, pos)


def kernel(x, c, ada_w, ada_b, norm1_g, norm2_g, pool_w, pool_scale, w_qkv, w_o, q_norm_g, k_norm_g,
           w_grp, b_grp, w_exp, b_exp, w_gate, w_up, w_down):
    depth = ada_w.shape[0]
    b, s, d = x.shape
    n_mixers = 2
    mod = _ada_call(c, ada_w, ada_b).reshape(depth, b, 6, 1, d)
    wshape = w_gate.shape
    w_gate_s = w_gate.reshape((-1,) + wshape[3:])
    w_up_s = w_up.reshape((-1,) + wshape[3:])
    w_down_s = w_down.reshape((-1,) + w_down.shape[3:])
    for i in range(depth):
        shift1, scale1, gate1, shift2, scale2, gate2 = (mod[i, :, k] for k in range(6))
        j = i // n_mixers
        if i % n_mixers == 0:
            x = _pool_call(x, shift1, scale1, gate1, norm1_g[i], pool_w[j], pool_scale[j])
        else:
            qkv = _qkv_call(x, shift1, scale1, norm1_g[i], w_qkv[j], q_norm_g[j], k_norm_g[j])
            o = _attn_call(qkv, d)
            x = _oproj_call(o, w_o[j], x, gate1)
        x = _moe(x, shift2, scale2, gate2, norm2_g[i], w_grp[i], b_grp[i], w_exp[i], b_exp[i],
                 w_gate_s, w_up_s, w_down_s, (i, depth))
    return x
```
